```python
import math
import jax, jax.numpy as jnp
from jax import lax
import numpy as np

D_MODEL = 1024
BATCH = 16
SEQ = 4096
DEPTH = 1

HEAD_DIM = 64
MOBA_HEADS = 8
RET_HEADS = 8
MOBA_WIDTH = MOBA_HEADS * HEAD_DIM
RET_WIDTH = RET_HEADS * HEAD_DIM
MIX_WIDTH = MOBA_WIDTH + RET_WIDTH
IN_COLS = 3 * MOBA_WIDTH + 4 * RET_WIDTH
MOBA_BLOCK = 256
MOBA_TOPK = 3
QUERY_BLOCK = 128
ROPE_THETA = 500000.0
ROPE_DIM = HEAD_DIM // 4
RET_ROPE_THETA = 10000.0
RET_CHUNK = 256
SEQ_ALIGN = 256
D_FF = ((8 * D_MODEL // 3 + 255) // 256) * 256
CONV_WIDTH = 3
NORM_EPS = 1e-6
GN_EPS = 1e-5
NEG_BIG = -1e9

kernel_name = "hymba_moba_retention_convffn"


def rmsnorm(x, g):
    xf = x.astype(jnp.float32)
    y = xf * lax.rsqrt(jnp.mean(xf * xf, axis=-1, keepdims=True) + NORM_EPS)
    return (y * g.astype(jnp.float32)).astype(x.dtype)


def rope(x, inv_freq):
    s = x.shape[2]
    half = inv_freq.shape[0]
    rot_dim = 2 * half
    ang = jnp.arange(s, dtype=jnp.float32)[:, None] * inv_freq[None, :]
    cos, sin = jnp.cos(ang), jnp.sin(ang)
    xr = x[..., :rot_dim].astype(jnp.float32)
    x1, x2 = xr[..., :half], xr[..., half:]
    rot = jnp.concatenate([x1 * cos - x2 * sin, x2 * cos + x1 * sin], axis=-1).astype(x.dtype)
    return jnp.concatenate([rot, x[..., rot_dim:]], axis=-1)


def moba_attention(q, k, v):
    b, h, s, dh = q.shape
    nb = s // MOBA_BLOCK
    nq = s // QUERY_BLOCK
    ksel = min(MOBA_TOPK, nb)
    scale = dh ** -0.5
    kb = k.reshape(b, h, nb, MOBA_BLOCK, dh)
    vb = v.reshape(b, h, nb, MOBA_BLOCK, dh)
    kbar = jnp.mean(kb.astype(jnp.float32), axis=3)
    gate = jnp.einsum('bhsd,bhnd->bhsn', q.astype(jnp.float32), kbar)
    qblk = jnp.arange(s) // MOBA_BLOCK
    past = jnp.arange(nb)[None, :] < qblk[:, None]
    gate = jnp.where(past, gate, NEG_BIG)
    _, idx = lax.top_k(lax.stop_gradient(gate), ksel)
    idx = jnp.clip(idx, 0, nb - 1)
    valid = idx < qblk[:, None]

    def to_qblocks(t):
        t = t.reshape((b, h, nq, QUERY_BLOCK) + t.shape[3:])
        t = jnp.moveaxis(t, 2, 1)
        return t.reshape((b * nq, h, QUERY_BLOCK) + t.shape[4:])

    q_x, idx_x, valid_x = to_qblocks(q), to_qblocks(idx), to_qblocks(valid)
    b_ix = jnp.repeat(jnp.arange(b), nq)
    qb_ix = jnp.tile(jnp.arange(nq), b)
    head_ix = jnp.arange(h)[:, None, None]
    q_off = jnp.arange(QUERY_BLOCK)
    k_off = jnp.arange(MOBA_BLOCK)

    def step(args):
        qq, ii, vv, bi, qi = args
        kb_b = kb[bi]
        vb_b = vb[bi]
        kg = kb_b[head_ix, ii]
        vg = vb_b[head_ix, ii]
        s_sel = jnp.einsum('hqd,hqjld->hqjl', qq, kg).astype(jnp.float32) * scale
        m_sel = jnp.broadcast_to(vv[..., None], s_sel.shape).reshape(h, QUERY_BLOCK, ksel * MOBA_BLOCK)
        s_sel = s_sel.reshape(h, QUERY_BLOCK, ksel * MOBA_BLOCK)
        own = (qi * QUERY_BLOCK) // MOBA_BLOCK
        ko = lax.dynamic_index_in_dim(kb_b, own, axis=1, keepdims=False)
        vo = lax.dynamic_index_in_dim(vb_b, own, axis=1, keepdims=False)
        s_own = jnp.einsum('hqd,hld->hql', qq, ko).astype(jnp.float32) * scale
        qpos = qi * QUERY_BLOCK + q_off
        kpos = own * MOBA_BLOCK + k_off
        m_own = jnp.broadcast_to((kpos[None, :] <= qpos[:, None])[None], s_own.shape)
        scores = jnp.concatenate([s_sel, s_own], axis=-1)
        mask = jnp.concatenate([m_sel, m_own], axis=-1)
        scores = jnp.where(mask, scores, NEG_BIG)
        mx = lax.stop_gradient(jnp.max(scores, axis=-1, keepdims=True))
        e = jnp.where(mask, jnp.exp(scores - mx), 0.0)
        p = e / jnp.sum(e, axis=-1, keepdims=True)
        p_sel = p[..., :ksel * MOBA_BLOCK].reshape(h, QUERY_BLOCK, ksel, MOBA_BLOCK).astype(v.dtype)
        p_own = p[..., ksel * MOBA_BLOCK:].astype(v.dtype)
        return (jnp.einsum('hqjl,hqjld->hqd', p_sel, vg)
                + jnp.einsum('hql,hld->hqd', p_own, vo))

    out = lax.map(step, (q_x, idx_x, valid_x, b_ix, qb_ix))
    out = out.reshape(b, nq, h, QUERY_BLOCK, dh).transpose(0, 1, 3, 2, 4)
    return out.reshape(b, s, h * dh)


def retention(q, k, v):
    b, h, s, dk = q.shape
    dv = v.shape[-1]
    c = RET_CHUNK
    n = s // c
    log_gamma = jnp.log(1.0 - 2.0 ** (-5.0 - jnp.arange(h, dtype=jnp.float32)))
    pos = jnp.arange(c, dtype=jnp.float32)
    diff = pos[:, None] - pos[None, :]
    inner_decay = jnp.where(diff[None] >= 0,
                            jnp.exp(jnp.maximum(diff, 0.0)[None] * log_gamma[:, None, None]),
                            0.0)
    cross_decay = jnp.exp((pos + 1.0)[None, :] * log_gamma[:, None])
    state_decay = jnp.exp((c - 1.0 - pos)[None, :] * log_gamma[:, None])
    chunk_decay = jnp.exp(c * log_gamma)

    def to_chunks(t):
        return t.reshape(b, h, n, c, t.shape[-1]).transpose(2, 0, 1, 3, 4)

    def step(state, inp):
        qi, ki, vi = (t.astype(jnp.float32) for t in inp)
        ki = ki * (dk ** -0.5)
        scores = jnp.einsum('bhqd,bhkd->bhqk', qi, ki) * inner_decay
        o = (jnp.einsum('bhqk,bhke->bhqe', scores, vi)
             + jnp.einsum('bhqd,bhde->bhqe', qi, state) * cross_decay[None, :, :, None])
        state = (state * chunk_decay[None, :, None, None]
                 + jnp.einsum('bhkd,bhke->bhde', ki * state_decay[None, :, :, None], vi))
        return state, o

    state0 = jnp.zeros((b, h, dk, dv), jnp.float32)
    _, o = lax.scan(step, state0, (to_chunks(q), to_chunks(k), to_chunks(v)))
    return o.transpose(1, 2, 0, 3, 4).reshape(b, h, s, dv)


def causal_depthwise_conv(u, w, bias):
    c = u.shape[-1]
    y = lax.conv_general_dilated(u, w[:, None, :].astype(u.dtype), window_strides=(1,),
                                 padding=((CONV_WIDTH - 1, 0),),
                                 dimension_numbers=('NWC', 'WIO', 'NWC'),
                                 feature_group_count=c)
    return y + bias.astype(u.dtype)


def split_heads(t, n_heads):
    b, s, _ = t.shape
    return t.reshape(b, s, n_heads, HEAD_DIM).transpose(0, 2, 1, 3)


def hybrid_layer(x, ln1, w_in, gn_gain, w_out, ln2, w_up, conv_w, conv_b, w_down):
    b, s, _ = x.shape
    s_pad = -(-s // SEQ_ALIGN) * SEQ_ALIGN
    h = rmsnorm(x, ln1)
    p = h @ w_in
    cuts = [int(c) for c in np.cumsum([MOBA_WIDTH] * 3 + [RET_WIDTH] * 3)]
    qa, ka, va, qr, kr, vr, gr = jnp.split(p, cuts, axis=-1)
    pad = lambda t: jnp.pad(t, ((0, 0), (0, 0), (0, s_pad - s), (0, 0)))
    inv_a = ROPE_THETA ** (-jnp.arange(ROPE_DIM // 2, dtype=jnp.float32) / (ROPE_DIM // 2))
    qa = rope(pad(split_heads(qa, MOBA_HEADS)), inv_a)
    ka = rope(pad(split_heads(ka, MOBA_HEADS)), inv_a)
    va = pad(split_heads(va, MOBA_HEADS))
    y_a = moba_attention(qa, ka, va)[:, :s]
    inv_r = 1.0 / (RET_ROPE_THETA ** jnp.linspace(0.0, 1.0, HEAD_DIM // 2, dtype=jnp.float32))
    qr = rope(pad(split_heads(qr, RET_HEADS)), inv_r)
    kr = rope(pad(split_heads(kr, RET_HEADS)), inv_r)
    vr = pad(split_heads(vr, RET_HEADS))
    o = retention(qr, kr, vr)[:, :, :s]
    mu = jnp.mean(o, axis=-1, keepdims=True)
    var = jnp.mean(jnp.square(o - mu), axis=-1, keepdims=True)
    o = ((o - mu) * lax.rsqrt(var + GN_EPS)).transpose(0, 2, 1, 3).reshape(b, s, RET_WIDTH)
    y_r = (jax.nn.silu(gr.astype(jnp.float32)) * o * gn_gain.astype(jnp.float32)).astype(x.dtype)
    x = x + jnp.concatenate([y_a.astype(x.dtype), y_r], axis=-1) @ w_out
    h2 = rmsnorm(x, ln2)
    gate, up = jnp.split(h2 @ w_up, 2, axis=-1)
    gate = causal_depthwise_conv(gate, conv_w, conv_b)
    return x + (jax.nn.silu(gate) * up) @ w_down


def setup_inputs(seed: int = 0) -> dict:
    key = jax.random.key(seed)
    ks = jax.random.split(key, 12)
    f32 = jnp.float32
    nrm = lambda k, shape, fan_in: jax.random.normal(k, shape, f32) * (fan_in ** -0.5)
    return {
        "x": jax.random.normal(ks[0], (BATCH, SEQ, D_MODEL), f32),
        "ln1": 1.0 + 0.02 * jax.random.normal(ks[1], (DEPTH, D_MODEL), f32),
        "w_in": nrm(ks[2], (DEPTH, D_MODEL, IN_COLS), D_MODEL),
        "gn_gain": 1.0 + 0.02 * jax.random.normal(ks[3], (DEPTH, RET_WIDTH), f32),
        "w_out": nrm(ks[4], (DEPTH, MIX_WIDTH, D_MODEL), MIX_WIDTH),
        "ln2": 1.0 + 0.02 * jax.random.normal(ks[5], (DEPTH, D_MODEL), f32),
        "w_up": nrm(ks[6], (DEPTH, D_MODEL, 2 * D_FF), D_MODEL),
        "conv_w": nrm(ks[7], (DEPTH, CONV_WIDTH, D_FF), CONV_WIDTH),
        "conv_b": 0.02 * jax.random.normal(ks[8], (DEPTH, D_FF), f32),
        "w_down": nrm(ks[9], (DEPTH, D_FF, D_MODEL), D_FF),
        "ln_f": 1.0 + 0.02 * jax.random.normal(ks[10], (D_MODEL,), f32),
    }


def reference(x, ln1, w_in, gn_gain, w_out, ln2, w_up, conv_w, conv_b, w_down, ln_f):
    for l in range(DEPTH):
        x = hybrid_layer(x, ln1[l], w_in[l], gn_gain[l], w_out[l], ln2[l],
                         w_up[l], conv_w[l], conv_b[l], w_down[l])
    return rmsnorm(x, ln_f)
```

```python
import functools

import numpy as np
import jax
import jax.numpy as jnp
from jax import lax
from jax.experimental import pallas as pl
from jax.experimental.pallas import tpu as pltpu

D_MODEL = 1024
HEAD_DIM = 64
MOBA_HEADS = 8
RET_HEADS = 8
WIDTH = 512
IN_COLS = 7 * WIDTH
MOBA_BLOCK = 256
MOBA_TOPK = 3
ROPE_THETA = 500000.0
ROPE_DIM = HEAD_DIM // 4
RET_ROPE_THETA = 10000.0
RET_CHUNK = 256
D_FF = 2816
NORM_EPS = 1e-6
GN_EPS = 1e-5
NEG_BIG = -1e9

LANES = 128
HEAD_PAIRS = WIDTH // LANES
SUBLANES = 8
VMEM_LIMIT = 56 * 1024 * 1024

IN_TILE = 512
FFN_TILE = 512
FFN_CHUNK = 256
FFN_NCHUNK = D_FF // FFN_CHUNK

F32 = jnp.float32
BF16 = jnp.bfloat16
NT_DIMS = (((1,), (1,)), ((), ()))
TN_DIMS = (((0,), (0,)), ((), ()))


def _rmsnorm(x, g):
    return x * lax.rsqrt(jnp.mean(x * x, axis=-1, keepdims=True) + NORM_EPS) * g


def _resident(shape):
    zeros = (0,) * len(shape)
    return pl.BlockSpec(shape, lambda *_: zeros, pipeline_mode=pl.Buffered(1))


def _rope_tables(seq, inv_freq, half):
    ang = jnp.arange(seq, dtype=F32)[:, None] * inv_freq[None, :]
    cos, sin = jnp.cos(ang), jnp.sin(ang)
    pad = HEAD_DIM - 2 * half
    ones = jnp.ones((seq, pad), F32)
    zeros = jnp.zeros((seq, pad), F32)
    zh = jnp.zeros((seq, half), F32)
    c = jnp.concatenate([cos, cos, ones], axis=1)
    su = jnp.concatenate([-sin, zh, zeros], axis=1)
    sd = jnp.concatenate([zh, sin, zeros], axis=1)
    two = lambda t: jnp.concatenate([t, t], axis=1)
    return two(c), two(su), two(sd)


def _inproj_kernel(x_ref, ln_ref, w_ref, ca_ref, sau_ref, sad_ref, cr_ref, sru_ref, srd_ref,
                   p_ref):
    h = _rmsnorm(x_ref[0], ln_ref[...]).astype(BF16)

    def rope(y, c, su, sd, half):
        outs = []
        for g in range(WIDTH // LANES):
            yg = y[:, g * LANES:(g + 1) * LANES]
            up = pltpu.roll(yg, LANES - half, 1)
            dn = pltpu.roll(yg, half, 1)
            outs.append(yg * c + up * su + dn * sd)
        return jnp.concatenate(outs, axis=1)

    for slab in range(IN_COLS // WIDTH):
        cols = slice(slab * WIDTH, (slab + 1) * WIDTH)
        y = jnp.dot(h, w_ref[:, cols], preferred_element_type=F32)
        if slab in (0, 1):
            y = rope(y, ca_ref[...], sau_ref[...], sad_ref[...], ROPE_DIM // 2)
        elif slab in (3, 4):
            y = rope(y, cr_ref[...], sru_ref[...], srd_ref[...], HEAD_DIM // 2)
        p_ref[0, :, cols] = y.astype(BF16)


def _inproj(x, ln, w_bf16, tabs_a, tabs_r):
    b, s, _ = x.shape
    tm = IN_TILE
    tab = pl.BlockSpec((tm, LANES), lambda si, bi: (si, 0))
    return pl.pallas_call(
        _inproj_kernel,
        grid=(s // tm, b),
        in_specs=[
            pl.BlockSpec((1, tm, D_MODEL), lambda si, bi: (bi, si, 0)),
            _resident((1, D_MODEL)),
            _resident((D_MODEL, IN_COLS)),
            tab, tab, tab, tab, tab, tab,
        ],
        out_specs=pl.BlockSpec((1, tm, IN_COLS), lambda si, bi: (bi, si, 0)),
        out_shape=jax.ShapeDtypeStruct((b, s, IN_COLS), BF16),
        compiler_params=pltpu.CompilerParams(
            dimension_semantics=("arbitrary", "arbitrary"), vmem_limit_bytes=VMEM_LIMIT),
        name="inproj",
    )(x, ln.reshape(1, D_MODEL), w_bf16, *tabs_a, *tabs_r)


def _moba_kernel(q_ref, k_ref, v_ref, o_ref, kaug_ref, vaug_ref, kbar_ref, acc_ref, m_ref,
                 *, nblocks):
    i = pl.program_id(2)
    blk = MOBA_BLOCK
    lane = lax.broadcasted_iota(jnp.int32, (blk, LANES), 1)
    head0 = lane < HEAD_DIM
    lane_b = lane.astype(F32).astype(BF16)
    head0_b = lane_b < HEAD_DIM
    one = jnp.ones((blk, LANES), BF16)
    zero = jnp.zeros((blk, LANES), BF16)

    @pl.when(i == 0)
    def _():
        for n in range(nblocks):
            rows = slice(n * blk, (n + 1) * blk)
            k = k_ref[0, rows, :]
            v = v_ref[0, rows, :]
            kaug_ref[0, rows, :] = jnp.where(head0_b, k, jnp.where(lane_b == HEAD_DIM + n, one, zero))
            kaug_ref[1, rows, :] = jnp.where(head0_b, jnp.where(lane_b == n, one, zero), k)
            vaug_ref[0, rows, :] = jnp.where(head0_b, v, one)
            vaug_ref[1, rows, :] = jnp.where(head0_b, one, v)
            kbar_ref[n:n + 1, :] = jnp.sum(k.astype(F32), axis=0, keepdims=True) / blk

    q = q_ref[0]
    kbar = kbar_ref[...]
    kb_hi = kbar.astype(BF16)
    kb_lo = (kbar - kb_hi.astype(F32)).astype(BF16)
    n_iota = lax.broadcasted_iota(jnp.int32, (nblocks, blk), 0)
    past = n_iota < i
    pens = []
    for e in range(2):
        qe = jnp.where(head0_b, q, zero) if e == 0 else jnp.where(head0_b, zero, q)
        g = (lax.dot_general(kb_hi, qe, NT_DIMS, preferred_element_type=F32)
             + lax.dot_general(kb_lo, qe, NT_DIMS, preferred_element_type=F32))
        g = jnp.where(past, g, NEG_BIG)
        cnt = jnp.zeros((nblocks, blk), F32)
        for m in range(nblocks):
            row = g[m:m + 1, :]
            ahead = (row > g) | ((row == g) & (n_iota > m))
            cnt = cnt + jnp.where(ahead, 1.0, 0.0)
        keep = ((cnt < MOBA_TOPK) & past) | (n_iota == i)
        pens.append(jnp.where(keep, 0.0, NEG_BIG))
    fill = jnp.zeros((HEAD_DIM - nblocks, blk), F32)
    pen_t = jnp.concatenate([pens[1], fill, pens[0], fill], axis=0)
    pen = pen_t.T.astype(BF16)
    q_aug = (jnp.where(head0_b, q, pen), jnp.where(head0_b, pen, q))

    def scores(e, start):
        kb = kaug_ref[e, pl.ds(start, blk), :]
        return lax.dot_general(q_aug[e], kb, NT_DIMS, preferred_element_type=F32)

    own = pl.multiple_of(i * blk, blk)
    r_iota = lax.broadcasted_iota(jnp.int32, (blk, blk), 0)
    c_iota = lax.broadcasted_iota(jnp.int32, (blk, blk), 1)
    causal = c_iota <= r_iota
    for e in range(2):
        s = jnp.where(causal, scores(e, own), NEG_BIG)
        m0 = jnp.max(s, axis=1, keepdims=True)
        p = jnp.exp(s - m0).astype(BF16)
        acc_ref[e] = jnp.dot(p, vaug_ref[e, pl.ds(own, blk), :], preferred_element_type=F32)
        m_ref[e] = jnp.broadcast_to(m0, (blk, LANES))

    def body(j, carry):
        start = pl.multiple_of(j * blk, blk)
        for e in range(2):
            s = scores(e, start)
            m_prev = m_ref[e]
            m_new = jnp.maximum(m_prev, jnp.max(s, axis=1, keepdims=True))
            alpha = jnp.exp(m_prev - m_new)
            p = jnp.exp(s - jnp.concatenate([m_new, m_new], axis=1)).astype(BF16)
            pv = jnp.dot(p, vaug_ref[e, pl.ds(start, blk), :], preferred_element_type=F32)
            acc_ref[e] = alpha * acc_ref[e] + pv
            m_ref[e] = m_new
        return carry

    lax.fori_loop(0, i, body, 0)

    a0 = acc_ref[0]
    a1 = acc_ref[1]
    out = jnp.where(head0, a0 / pltpu.roll(a0, HEAD_DIM, 1), a1 / pltpu.roll(a1, HEAD_DIM, 1))
    o_ref[0] = out.astype(BF16)


def _moba(p):
    b, s, _ = p.shape
    nblocks = s // MOBA_BLOCK
    assert nblocks <= HEAD_DIM and nblocks % SUBLANES == 0
    qspec = pl.BlockSpec((1, MOBA_BLOCK, LANES), lambda bi, hp, i: (bi, i, hp))
    kspec = pl.BlockSpec((1, s, LANES), lambda bi, hp, i: (bi, 0, HEAD_PAIRS + hp))
    vspec = pl.BlockSpec((1, s, LANES), lambda bi, hp, i: (bi, 0, 2 * HEAD_PAIRS + hp))
    return pl.pallas_call(
        functools.partial(_moba_kernel, nblocks=nblocks),
        grid=(b, HEAD_PAIRS, nblocks),
        in_specs=[qspec, kspec, vspec],
        out_specs=pl.BlockSpec((1, MOBA_BLOCK, LANES), lambda bi, hp, i: (bi, i, hp)),
        out_shape=jax.ShapeDtypeStruct((b, s, WIDTH), BF16),
        scratch_shapes=[
            pltpu.VMEM((2, s, LANES), BF16),
            pltpu.VMEM((2, s, LANES), BF16),
            pltpu.VMEM((nblocks, LANES), F32),
            pltpu.VMEM((2, MOBA_BLOCK, LANES), F32),
            pltpu.VMEM((2, MOBA_BLOCK, LANES), F32),
        ],
        compiler_params=pltpu.CompilerParams(
            dimension_semantics=("arbitrary", "arbitrary", "arbitrary"),
            vmem_limit_bytes=VMEM_LIMIT),
        name="moba",
    )(p, p, p)


def _retention_tables():
    c = RET_CHUNK
    log_gamma = jnp.log(1.0 - 2.0 ** (-5.0 - jnp.arange(RET_HEADS, dtype=F32)))
    pos = jnp.arange(c, dtype=F32)
    diff = pos[:, None] - pos[None, :]
    inner = jnp.where(diff[None] >= 0,
                      jnp.exp(jnp.maximum(diff, 0.0)[None] * log_gamma[:, None, None]), 0.0)
    cross = jnp.exp((pos + 1.0)[None, :] * log_gamma[:, None])
    sdec = jnp.exp((c - 1.0 - pos)[None, :] * log_gamma[:, None])
    chunk = jnp.exp(c * log_gamma)

    def lanes(t):
        t = jnp.repeat(t[:, :, None], HEAD_DIM, axis=2)
        t = t.reshape(HEAD_PAIRS, 2, c, HEAD_DIM).transpose(0, 2, 1, 3)
        return t.reshape(HEAD_PAIRS, c, LANES)

    same_head = (jnp.arange(LANES)[:, None] // HEAD_DIM) == (jnp.arange(LANES)[None, :] // HEAD_DIM)
    bd = same_head.astype(F32)
    kdec = jnp.repeat(chunk.reshape(HEAD_PAIRS, 2), HEAD_DIM, axis=1)[:, :, None] * bd[None]
    return inner, lanes(cross), lanes(sdec), kdec, bd


def _retention_kernel(q_ref, k_ref, v_ref, g_ref, idec_ref, cdec_ref, sdec_ref, kdec_ref, bd_ref,
                      gain_ref, o_ref, state_ref):
    c = pl.program_id(2)
    lane = lax.broadcasted_iota(jnp.int32, (RET_CHUNK, LANES), 1)
    head0 = lane < HEAD_DIM
    head0_b = lane.astype(F32).astype(BF16) < HEAD_DIM
    zero = jnp.zeros((RET_CHUNK, LANES), BF16)

    @pl.when(c == 0)
    def _():
        state_ref[...] = jnp.zeros_like(state_ref)

    q = q_ref[0]
    k = k_ref[0]
    v = v_ref[0]
    state = state_ref[...]
    o_cross = jnp.dot(q, state.astype(BF16), preferred_element_type=F32) * cdec_ref[0]
    parts = []
    for e in range(2):
        qe = jnp.where(head0_b, q, zero) if e == 0 else jnp.where(head0_b, zero, q)
        s = lax.dot_general(qe, k, NT_DIMS, preferred_element_type=F32) * idec_ref[e]
        parts.append(jnp.dot(s.astype(BF16), v, preferred_element_type=F32))
    o = jnp.where(head0, parts[0], parts[1]) + o_cross

    kd = (k.astype(F32) * sdec_ref[0]).astype(BF16)
    upd = lax.dot_general(kd, v, TN_DIMS, preferred_element_type=F32)
    state_ref[...] = state * kdec_ref[0] + upd * bd_ref[...]

    def head_mean(t):
        s0 = jnp.sum(jnp.where(head0, t, 0.0), axis=1, keepdims=True)
        s1 = jnp.sum(jnp.where(head0, 0.0, t), axis=1, keepdims=True)
        return jnp.where(head0, s0, s1) / HEAD_DIM

    d = o - head_mean(o)
    on = d * lax.rsqrt(head_mean(d * d) + GN_EPS)
    g = g_ref[0].astype(F32)
    y = g * (1.0 / (1.0 + jnp.exp(-g))) * on * gain_ref[...]
    o_ref[0] = y.astype(BF16)


def _retention(p, gn_gain, tables):
    b, s, _ = p.shape
    c = RET_CHUNK
    inner, cross, sdec, kdec, bd = tables

    def col(group):
        return pl.BlockSpec((1, c, LANES), lambda bi, hp, ci: (bi, ci, group * HEAD_PAIRS + hp))

    pair = lambda shape: pl.BlockSpec(shape, lambda bi, hp, ci: (hp, 0, 0))
    return pl.pallas_call(
        _retention_kernel,
        grid=(b, HEAD_PAIRS, s // c),
        in_specs=[
            col(3), col(4), col(5), col(6),
            pl.BlockSpec((2, c, c), lambda bi, hp, ci: (hp, 0, 0)),
            pair((1, c, LANES)), pair((1, c, LANES)), pair((1, LANES, LANES)),
            _resident((LANES, LANES)),
            pl.BlockSpec((1, LANES), lambda bi, hp, ci: (0, hp)),
        ],
        out_specs=pl.BlockSpec((1, c, LANES), lambda bi, hp, ci: (bi, ci, hp)),
        out_shape=jax.ShapeDtypeStruct((b, s, WIDTH), BF16),
        scratch_shapes=[pltpu.VMEM((LANES, LANES), F32)],
        compiler_params=pltpu.CompilerParams(
            dimension_semantics=("arbitrary", "arbitrary", "arbitrary"),
            vmem_limit_bytes=VMEM_LIMIT),
        name="retention",
    )(p, p, p, p, inner, cross, sdec, kdec, bd, gn_gain.reshape(1, WIDTH))


def _ffn_kernel(x_ref, ya_ref, yr_ref, woa_ref, wor_ref, ln2_ref, wg_ref, wu_ref, cw_ref, cb_ref,
                wd_ref, lnf_ref, o_ref, x1_ref, h2_ref, acc_ref, carry_ref, *, final_norm):
    tm = x_ref.shape[1]

    @pl.when(pl.program_id(1) == 0)
    def _():
        carry_ref[...] = jnp.zeros_like(carry_ref)

    x1 = (x_ref[0]
          + jnp.dot(ya_ref[0], woa_ref[...], preferred_element_type=F32)
          + jnp.dot(yr_ref[0], wor_ref[...], preferred_element_type=F32))
    x1_ref[...] = x1
    h2_ref[...] = _rmsnorm(x1, ln2_ref[...]).astype(BF16)
    acc_ref[...] = jnp.zeros_like(acc_ref)
    rows = lax.broadcasted_iota(jnp.int32, (tm, FFN_CHUNK), 0)
    row0 = rows == 0
    row1 = rows == 1

    def body(c, carry):
        h2 = h2_ref[...]
        g = jnp.dot(h2, wg_ref[c], preferred_element_type=F32)
        u = jnp.dot(h2, wu_ref[c], preferred_element_type=F32)
        prev = carry_ref[c]
        carry_ref[c] = g[tm - SUBLANES:, :]
        p1 = prev[SUBLANES - 1:SUBLANES, :]
        p2 = prev[SUBLANES - 2:SUBLANES - 1, :]
        g1 = jnp.where(row0, p1, pltpu.roll(g, 1, 0))
        g2 = jnp.where(row0, p2, jnp.where(row1, p1, pltpu.roll(g, 2, 0)))
        cw = cw_ref[c]
        gc = cw[0:1, :] * g2 + cw[1:2, :] * g1 + cw[2:3, :] * g + cb_ref[c]
        a = (gc * (1.0 / (1.0 + jnp.exp(-gc))) * u).astype(BF16)
        acc_ref[...] += jnp.dot(a, wd_ref[c], preferred_element_type=F32)
        return carry

    lax.fori_loop(0, FFN_NCHUNK, body, 0)
    x2 = x1_ref[...] + acc_ref[...]
    if final_norm:
        x2 = _rmsnorm(x2, lnf_ref[...])
    o_ref[0] = x2


def _ffn(x, ya, yr, woa, wor, ln2, wg, wu, cw, cb, wd, ln_f, final_norm):
    b, s, _ = x.shape
    tm = FFN_TILE
    tok = lambda width: pl.BlockSpec((1, tm, width), lambda bi, ti: (bi, ti, 0))
    return pl.pallas_call(
        functools.partial(_ffn_kernel, final_norm=final_norm),
        grid=(b, s // tm),
        in_specs=[
            tok(D_MODEL), tok(WIDTH), tok(WIDTH),
            _resident((WIDTH, D_MODEL)), _resident((WIDTH, D_MODEL)),
            _resident((1, D_MODEL)),
            _resident((FFN_NCHUNK, D_MODEL, FFN_CHUNK)), _resident((FFN_NCHUNK, D_MODEL, FFN_CHUNK)),
            _resident((FFN_NCHUNK, 3, FFN_CHUNK)), _resident((FFN_NCHUNK, 1, FFN_CHUNK)),
            _resident((FFN_NCHUNK, FFN_CHUNK, D_MODEL)),
            _resident((1, D_MODEL)),
        ],
        out_specs=tok(D_MODEL),
        out_shape=jax.ShapeDtypeStruct((b, s, D_MODEL), F32),
        scratch_shapes=[
            pltpu.VMEM((tm, D_MODEL), F32),
            pltpu.VMEM((tm, D_MODEL), BF16),
            pltpu.VMEM((tm, D_MODEL), F32),
            pltpu.VMEM((FFN_NCHUNK, SUBLANES, FFN_CHUNK), F32),
        ],
        compiler_params=pltpu.CompilerParams(
            dimension_semantics=("arbitrary", "arbitrary"), vmem_limit_bytes=VMEM_LIMIT),
        name="ffn",
    )(x, ya, yr, woa, wor, ln2.reshape(1, D_MODEL), wg, wu, cw, cb, wd, ln_f.reshape(1, D_MODEL))


def _chunk_cols(w):
    k = w.shape[0]
    return w.reshape(k, FFN_NCHUNK, FFN_CHUNK).transpose(1, 0, 2)


def kernel(x, ln1, w_in, gn_gain, w_out, ln2, w_up, conv_w, conv_b, w_down, ln_f):
    b, s, d = x.shape
    depth = w_in.shape[0]
    assert d == D_MODEL and s % max(IN_TILE, FFN_TILE, MOBA_BLOCK, RET_CHUNK) == 0

    inv_a = ROPE_THETA ** (-jnp.arange(ROPE_DIM // 2, dtype=F32) / (ROPE_DIM // 2))
    inv_r = 1.0 / (RET_ROPE_THETA ** jnp.linspace(0.0, 1.0, HEAD_DIM // 2, dtype=F32))
    tabs_a = _rope_tables(s, inv_a, ROPE_DIM // 2)
    tabs_r = _rope_tables(s, inv_r, HEAD_DIM // 2)
    ret_tables = _retention_tables()
    scale = HEAD_DIM ** -0.5
    col_scale = jnp.ones((IN_COLS,), F32).at[0:WIDTH].set(scale).at[4 * WIDTH:5 * WIDTH].set(scale)

    for l in range(depth):
        w_in_l = (w_in[l] * col_scale[None, :]).astype(BF16)
        p = _inproj(x, ln1[l], w_in_l, tabs_a, tabs_r)
        ya = _moba(p)
        yr = _retention(p, gn_gain[l], ret_tables)
        wo = w_out[l].astype(BF16)
        x = _ffn(x, ya, yr, wo[:WIDTH], wo[WIDTH:], ln2[l],
                 _chunk_cols(w_up[l][:, :D_FF]).astype(BF16),
                 _chunk_cols(w_up[l][:, D_FF:]).astype(BF16),
                 _chunk_cols(conv_w[l]), _chunk_cols(conv_b[l][None, :]),
                 w_down[l].astype(BF16).reshape(FFN_NCHUNK, FFN_CHUNK, D_MODEL),
                 ln_f, final_norm=(l == depth - 1))
    return x
```

```python
import functools

import jax
import jax.numpy as jnp
from jax import lax
from jax.experimental import pallas as pl
from jax.experimental.pallas import tpu as pltpu

D_MODEL = 1024
HEAD_DIM = 64
MOBA_HEADS = 8
RET_HEADS = 8
WIDTH = 512
IN_COLS = 7 * WIDTH
MOBA_BLOCK = 256
MOBA_TOPK = 3
ROPE_THETA = 500000.0
ROPE_DIM = HEAD_DIM // 4
RET_ROPE_THETA = 10000.0
RET_CHUNK = 256
D_FF = 2816
NORM_EPS = 1e-6
GN_EPS = 1e-5
NEG_BIG = -1e9

LANES = 128
HEAD_PAIRS = WIDTH // LANES
SUBLANES = 8
VMEM_LIMIT = 56 * 1024 * 1024

IN_TILE = 512
FFN_TILE = 512
FFN_CHUNK = 256
FFN_NCHUNK = D_FF // FFN_CHUNK
MOBA_UNROLL = 5

F32 = jnp.float32
BF16 = jnp.bfloat16
NT_DIMS = (((1,), (1,)), ((), ()))
TN_DIMS = (((0,), (0,)), ((), ()))


def _rmsnorm(x, g):
    return x * lax.rsqrt(jnp.mean(x * x, axis=-1, keepdims=True) + NORM_EPS) * g


def _resident(shape):
    zeros = (0,) * len(shape)
    return pl.BlockSpec(shape, lambda *_: zeros, pipeline_mode=pl.Buffered(1))


def _rope_tables(seq, inv_freq, half):
    ang = jnp.arange(seq, dtype=F32)[:, None] * inv_freq[None, :]
    cos, sin = jnp.cos(ang), jnp.sin(ang)
    pad = HEAD_DIM - 2 * half
    ones = jnp.ones((seq, pad), F32)
    zeros = jnp.zeros((seq, pad), F32)
    zh = jnp.zeros((seq, half), F32)
    c = jnp.concatenate([cos, cos, ones], axis=1)
    su = jnp.concatenate([-sin, zh, zeros], axis=1)
    sd = jnp.concatenate([zh, sin, zeros], axis=1)
    two = lambda t: jnp.concatenate([t, t], axis=1)
    return two(c), two(su), two(sd)


def _inproj_kernel(x_ref, ln_ref, w_ref, ca_ref, sau_ref, sad_ref, cr_ref, sru_ref, srd_ref,
                   p_ref):
    h = _rmsnorm(x_ref[0], ln_ref[...]).astype(BF16)

    def rope(y, c, su, sd, half):
        outs = []
        for g in range(WIDTH // LANES):
            yg = y[:, g * LANES:(g + 1) * LANES]
            up = pltpu.roll(yg, LANES - half, 1)
            dn = pltpu.roll(yg, half, 1)
            outs.append(yg * c + up * su + dn * sd)
        return jnp.concatenate(outs, axis=1)

    for slab in range(IN_COLS // WIDTH):
        cols = slice(slab * WIDTH, (slab + 1) * WIDTH)
        y = jnp.dot(h, w_ref[:, cols], preferred_element_type=F32)
        if slab in (0, 1):
            y = rope(y, ca_ref[...], sau_ref[...], sad_ref[...], ROPE_DIM // 2)
        elif slab in (3, 4):
            y = rope(y, cr_ref[...], sru_ref[...], srd_ref[...], HEAD_DIM // 2)
        p_ref[0, :, cols] = y.astype(BF16)


def _inproj(x, ln, w_bf16, tabs_a, tabs_r):
    b, s, _ = x.shape
    tm = IN_TILE
    tab = pl.BlockSpec((tm, LANES), lambda si, bi: (si, 0))
    return pl.pallas_call(
        _inproj_kernel,
        grid=(s // tm, b),
        in_specs=[
            pl.BlockSpec((1, tm, D_MODEL), lambda si, bi: (bi, si, 0)),
            _resident((1, D_MODEL)),
            _resident((D_MODEL, IN_COLS)),
            tab, tab, tab, tab, tab, tab,
        ],
        out_specs=pl.BlockSpec((1, tm, IN_COLS), lambda si, bi: (bi, si, 0)),
        out_shape=jax.ShapeDtypeStruct((b, s, IN_COLS), BF16),
        compiler_params=pltpu.CompilerParams(
            dimension_semantics=("arbitrary", "arbitrary"), vmem_limit_bytes=VMEM_LIMIT),
        name="inproj",
    )(x, ln.reshape(1, D_MODEL), w_bf16, *tabs_a, *tabs_r)


def _moba_kernel(qa_ref, qb_ref, k_ref, v_ref, oa_ref, ob_ref, kaug_ref, vaug_ref, kbar_ref,
                 qaug_ref, s_ref, mx_ref, m_ref, acc_ref, *, nblocks):
    p = pl.program_id(2)
    blk = MOBA_BLOCK
    npast = nblocks - 1
    q_blocks = (p, npast - p)
    lane = lax.broadcasted_iota(jnp.int32, (blk, LANES), 1)
    head0 = lane < HEAD_DIM
    lane_b = lane.astype(F32).astype(BF16)
    head0_b = lane_b < HEAD_DIM
    one = jnp.ones((blk, LANES), BF16)
    zero = jnp.zeros((blk, LANES), BF16)

    @pl.when(p == 0)
    def _():
        for n in range(nblocks):
            rows = slice(n * blk, (n + 1) * blk)
            k = k_ref[0, rows, :]
            v = v_ref[0, rows, :]
            kaug_ref[0, rows, :] = jnp.where(head0_b, k, jnp.where(lane_b == HEAD_DIM + n, one, zero))
            kaug_ref[1, rows, :] = jnp.where(head0_b, jnp.where(lane_b == n, one, zero), k)
            vaug_ref[0, rows, :] = jnp.where(head0_b, v, one)
            vaug_ref[1, rows, :] = jnp.where(head0_b, one, v)
            kbar_ref[n:n + 1, :] = jnp.sum(k.astype(F32), axis=0, keepdims=True) / blk

    kbar = kbar_ref[...]
    kb_hi = kbar.astype(BF16)
    kb_lo = (kbar - kb_hi.astype(F32)).astype(BF16)
    kb = jnp.concatenate([kb_hi, kb_lo], axis=0)
    n_iota = lax.broadcasted_iota(jnp.int32, (nblocks, blk), 0)
    fill = jnp.zeros((HEAD_DIM - nblocks, blk), F32)
    for w, q_ref in enumerate((qa_ref, qb_ref)):
        i = q_blocks[w]
        q = q_ref[0]
        past = n_iota < i
        pens = []
        for e in range(2):
            qe = jnp.where(head0_b, q, zero) if e == 0 else jnp.where(head0_b, zero, q)
            g2 = lax.dot_general(kb, qe, NT_DIMS, preferred_element_type=F32)
            g = jnp.where(past, g2[:nblocks] + g2[nblocks:], NEG_BIG)
            cnt = jnp.zeros((nblocks, blk), F32)
            for m in range(nblocks):
                row = g[m:m + 1, :]
                ahead = (row > g) | ((row == g) & (n_iota > m))
                cnt = cnt + jnp.where(ahead, 1.0, 0.0)
            keep = ((cnt < MOBA_TOPK) & past) | (n_iota == i)
            pens.append(jnp.where(keep, 0.0, NEG_BIG))
        pen_t = jnp.concatenate([pens[1], fill, pens[0], fill], axis=0)
        pen = pen_t.T.astype(BF16)
        qaug_ref[w, 0] = jnp.where(head0_b, q, pen)
        qaug_ref[w, 1] = jnp.where(head0_b, pen, q)

    def unit(t):
        w = (t >= p).astype(jnp.int32)
        return w, pl.multiple_of((t - w * p) * blk, blk)

    def scores(w, e, start):
        kblk = kaug_ref[e, pl.ds(start, blk), :]
        return lax.dot_general(qaug_ref[w, e], kblk, NT_DIMS, preferred_element_type=F32)

    def fold(s):
        return jnp.maximum(s[:, :LANES], s[:, LANES:])

    r_iota = lax.broadcasted_iota(jnp.int32, (blk, blk), 0)
    c_iota = lax.broadcasted_iota(jnp.int32, (blk, blk), 1)
    causal = c_iota <= r_iota
    for w in range(2):
        own = pl.multiple_of(q_blocks[w] * blk, blk)
        for e in range(2):
            s = jnp.where(causal, scores(w, e, own), NEG_BIG)
            s_ref[e, npast + w] = s
            mx_ref[w, e] = fold(s)

    def pass1(tt, carry):
        for u in range(MOBA_UNROLL):
            t = tt * MOBA_UNROLL + u
            w, start = unit(t)
            for e in range(2):
                s = scores(w, e, start)
                s_ref[e, t] = s
                mx_ref[w, e] = jnp.maximum(mx_ref[w, e], fold(s))
        return carry

    lax.fori_loop(0, npast // MOBA_UNROLL, pass1, 0)

    for w in range(2):
        for e in range(2):
            m_ref[w, e] = jnp.broadcast_to(jnp.max(mx_ref[w, e], axis=1, keepdims=True), (blk, LANES))

    def pv(w, e, t, start):
        m = m_ref[w, e]
        prob = jnp.exp(s_ref[e, t] - jnp.concatenate([m, m], axis=1)).astype(BF16)
        return jnp.dot(prob, vaug_ref[e, pl.ds(start, blk), :], preferred_element_type=F32)

    for w in range(2):
        own = pl.multiple_of(q_blocks[w] * blk, blk)
        for e in range(2):
            acc_ref[w, e] = pv(w, e, npast + w, own)

    def pass2(tt, carry):
        for u in range(MOBA_UNROLL):
            t = tt * MOBA_UNROLL + u
            w, start = unit(t)
            for e in range(2):
                acc_ref[w, e] += pv(w, e, t, start)
        return carry

    lax.fori_loop(0, npast // MOBA_UNROLL, pass2, 0)

    for w, o_ref in enumerate((oa_ref, ob_ref)):
        a0 = acc_ref[w, 0]
        a1 = acc_ref[w, 1]
        out = jnp.where(head0, a0 / pltpu.roll(a0, HEAD_DIM, 1), a1 / pltpu.roll(a1, HEAD_DIM, 1))
        o_ref[0] = out.astype(BF16)


def _moba(p):
    b, s, _ = p.shape
    nblocks = s // MOBA_BLOCK
    nhalf = nblocks // 2
    assert nblocks <= HEAD_DIM and nblocks % SUBLANES == 0 and (nblocks - 1) % MOBA_UNROLL == 0
    blk = MOBA_BLOCK
    kspec = pl.BlockSpec((1, s, LANES), lambda bi, hp, i: (bi, 0, HEAD_PAIRS + hp))
    vspec = pl.BlockSpec((1, s, LANES), lambda bi, hp, i: (bi, 0, 2 * HEAD_PAIRS + hp))
    half = jax.ShapeDtypeStruct((b, s // 2, WIDTH), BF16)
    return pl.pallas_call(
        functools.partial(_moba_kernel, nblocks=nblocks),
        grid=(b, HEAD_PAIRS, nhalf),
        in_specs=[
            pl.BlockSpec((1, blk, LANES), lambda bi, hp, i: (bi, i, hp)),
            pl.BlockSpec((1, blk, LANES), lambda bi, hp, i: (bi, nblocks - 1 - i, hp)),
            kspec, vspec,
        ],
        out_specs=[
            pl.BlockSpec((1, blk, LANES), lambda bi, hp, i: (bi, i, hp)),
            pl.BlockSpec((1, blk, LANES), lambda bi, hp, i: (bi, nhalf - 1 - i, hp)),
        ],
        out_shape=[half, half],
        scratch_shapes=[
            pltpu.VMEM((2, s, LANES), BF16),
            pltpu.VMEM((2, s, LANES), BF16),
            pltpu.VMEM((nblocks, LANES), F32),
            pltpu.VMEM((2, 2, blk, LANES), BF16),
            pltpu.VMEM((2, nblocks + 1, blk, blk), F32),
            pltpu.VMEM((2, 2, blk, LANES), F32),
            pltpu.VMEM((2, 2, blk, LANES), F32),
            pltpu.VMEM((2, 2, blk, LANES), F32),
        ],
        compiler_params=pltpu.CompilerParams(
            dimension_semantics=("arbitrary", "arbitrary", "arbitrary"),
            vmem_limit_bytes=VMEM_LIMIT),
        name="moba",
    )(p, p, p, p)


def _retention_tables():
    c = RET_CHUNK
    log_gamma = jnp.log(1.0 - 2.0 ** (-5.0 - jnp.arange(RET_HEADS, dtype=F32)))
    pos = jnp.arange(c, dtype=F32)
    diff = pos[:, None] - pos[None, :]
    inner = jnp.where(diff[None] >= 0,
                      jnp.exp(jnp.maximum(diff, 0.0)[None] * log_gamma[:, None, None]), 0.0)
    cross = jnp.exp((pos + 1.0)[None, :] * log_gamma[:, None])
    sdec = jnp.exp((c - 1.0 - pos)[None, :] * log_gamma[:, None])
    chunk = jnp.exp(c * log_gamma)

    def lanes(t):
        t = jnp.repeat(t[:, :, None], HEAD_DIM, axis=2)
        t = t.reshape(HEAD_PAIRS, 2, c, HEAD_DIM).transpose(0, 2, 1, 3)
        return t.reshape(HEAD_PAIRS, c, LANES)

    same_head = (jnp.arange(LANES)[:, None] // HEAD_DIM) == (jnp.arange(LANES)[None, :] // HEAD_DIM)
    bd = same_head.astype(F32)
    kdec = jnp.repeat(chunk.reshape(HEAD_PAIRS, 2), HEAD_DIM, axis=1)[:, :, None] * bd[None]
    return inner, lanes(cross), lanes(sdec), kdec, bd


def _retention_kernel(q_ref, k_ref, v_ref, g_ref, idec_ref, cdec_ref, sdec_ref, kdec_ref, bd_ref,
                      gain_ref, o_ref, state_ref):
    c = pl.program_id(2)
    lane = lax.broadcasted_iota(jnp.int32, (RET_CHUNK, LANES), 1)
    head0 = lane < HEAD_DIM
    head0_b = lane.astype(F32).astype(BF16) < HEAD_DIM
    zero = jnp.zeros((RET_CHUNK, LANES), BF16)

    @pl.when(c == 0)
    def _():
        state_ref[...] = jnp.zeros_like(state_ref)

    q = q_ref[0]
    k = k_ref[0]
    v = v_ref[0]
    state = state_ref[...]
    o_cross = jnp.dot(q, state.astype(BF16), preferred_element_type=F32) * cdec_ref[0]
    parts = []
    for e in range(2):
        qe = jnp.where(head0_b, q, zero) if e == 0 else jnp.where(head0_b, zero, q)
        s = lax.dot_general(qe, k, NT_DIMS, preferred_element_type=F32) * idec_ref[e]
        parts.append(jnp.dot(s.astype(BF16), v, preferred_element_type=F32))
    o = jnp.where(head0, parts[0], parts[1]) + o_cross

    kd = (k.astype(F32) * sdec_ref[0]).astype(BF16)
    upd = lax.dot_general(kd, v, TN_DIMS, preferred_element_type=F32)
    state_ref[...] = state * kdec_ref[0] + upd * bd_ref[...]

    def head_mean(t):
        s0 = jnp.sum(jnp.where(head0, t, 0.0), axis=1, keepdims=True)
        s1 = jnp.sum(jnp.where(head0, 0.0, t), axis=1, keepdims=True)
        return jnp.where(head0, s0, s1) / HEAD_DIM

    d = o - head_mean(o)
    on = d * lax.rsqrt(head_mean(d * d) + GN_EPS)
    g = g_ref[0].astype(F32)
    y = g * (1.0 / (1.0 + jnp.exp(-g))) * on * gain_ref[...]
    o_ref[0] = y.astype(BF16)


def _retention(p, gn_gain, tables):
    b, s, _ = p.shape
    c = RET_CHUNK
    inner, cross, sdec, kdec, bd = tables

    def col(group):
        return pl.BlockSpec((1, c, LANES), lambda bi, hp, ci: (bi, ci, group * HEAD_PAIRS + hp))

    pair = lambda shape: pl.BlockSpec(shape, lambda bi, hp, ci: (hp, 0, 0))
    return pl.pallas_call(
        _retention_kernel,
        grid=(b, HEAD_PAIRS, s // c),
        in_specs=[
            col(3), col(4), col(5), col(6),
            pl.BlockSpec((2, c, c), lambda bi, hp, ci: (hp, 0, 0)),
            pair((1, c, LANES)), pair((1, c, LANES)), pair((1, LANES, LANES)),
            _resident((LANES, LANES)),
            pl.BlockSpec((1, LANES), lambda bi, hp, ci: (0, hp)),
        ],
        out_specs=pl.BlockSpec((1, c, LANES), lambda bi, hp, ci: (bi, ci, hp)),
        out_shape=jax.ShapeDtypeStruct((b, s, WIDTH), BF16),
        scratch_shapes=[pltpu.VMEM((LANES, LANES), F32)],
        compiler_params=pltpu.CompilerParams(
            dimension_semantics=("arbitrary", "arbitrary", "arbitrary"),
            vmem_limit_bytes=VMEM_LIMIT),
        name="retention",
    )(p, p, p, p, inner, cross, sdec, kdec, bd, gn_gain.reshape(1, WIDTH))


def _ffn_kernel(x_ref, ya_lo_ref, ya_hi_ref, yr_ref, woa_ref, wor_ref, ln2_ref, wg_ref, wu_ref,
                cw_ref, cb_ref, wd_ref, lnf_ref, o_ref, x1_ref, h2_ref, acc_ref, carry_ref,
                *, final_norm):
    tm = x_ref.shape[1]

    @pl.when(pl.program_id(1) == 0)
    def _():
        carry_ref[...] = jnp.zeros_like(carry_ref)

    first_half = pl.program_id(1) < pl.num_programs(1) // 2

    @pl.when(first_half)
    def _():
        x1_ref[...] = jnp.dot(ya_lo_ref[0], woa_ref[...], preferred_element_type=F32)

    @pl.when(jnp.logical_not(first_half))
    def _():
        x1_ref[...] = jnp.dot(ya_hi_ref[0], woa_ref[...], preferred_element_type=F32)

    x1 = x_ref[0] + x1_ref[...] + jnp.dot(yr_ref[0], wor_ref[...], preferred_element_type=F32)
    x1_ref[...] = x1
    h2_ref[...] = _rmsnorm(x1, ln2_ref[...]).astype(BF16)
    acc_ref[...] = jnp.zeros_like(acc_ref)
    rows = lax.broadcasted_iota(jnp.int32, (tm, FFN_CHUNK), 0)
    row0 = rows == 0
    row1 = rows == 1

    def activation(c):
        h2 = h2_ref[...]
        g = jnp.dot(h2, wg_ref[c], preferred_element_type=F32)
        u = jnp.dot(h2, wu_ref[c], preferred_element_type=F32)
        prev = carry_ref[c]
        carry_ref[c] = g[tm - SUBLANES:, :]
        p1 = prev[SUBLANES - 1:SUBLANES, :]
        p2 = prev[SUBLANES - 2:SUBLANES - 1, :]
        g1 = jnp.where(row0, p1, pltpu.roll(g, 1, 0))
        g2 = jnp.where(row0, p2, jnp.where(row1, p1, pltpu.roll(g, 2, 0)))
        cw = cw_ref[c]
        gc = cw[0:1, :] * g2 + cw[1:2, :] * g1 + cw[2:3, :] * g + cb_ref[c]
        return (gc * (1.0 / (1.0 + jnp.exp(-gc))) * u).astype(BF16)

    def body(cc, carry):
        c0 = 2 * cc
        a0 = activation(c0)
        a1 = activation(c0 + 1)
        acc_ref[...] += (jnp.dot(a0, wd_ref[c0], preferred_element_type=F32)
                         + jnp.dot(a1, wd_ref[c0 + 1], preferred_element_type=F32))
        return carry

    lax.fori_loop(0, FFN_NCHUNK // 2, body, 0)
    for c in range(FFN_NCHUNK - FFN_NCHUNK % 2, FFN_NCHUNK):
        acc_ref[...] += jnp.dot(activation(c), wd_ref[c], preferred_element_type=F32)
    x2 = x1_ref[...] + acc_ref[...]
    if final_norm:
        x2 = _rmsnorm(x2, lnf_ref[...])
    o_ref[0] = x2


def _ffn(x, ya_lo, ya_hi, yr, woa, wor, ln2, wg, wu, cw, cb, wd, ln_f, final_norm):
    b, s, _ = x.shape
    tm = FFN_TILE
    nhalf = s // tm // 2
    tok = lambda width: pl.BlockSpec((1, tm, width), lambda bi, ti: (bi, ti, 0))
    lo = pl.BlockSpec((1, tm, WIDTH), lambda bi, ti: (bi, jnp.minimum(ti, nhalf - 1), 0))
    hi = pl.BlockSpec((1, tm, WIDTH), lambda bi, ti: (bi, jnp.maximum(ti - nhalf, 0), 0))
    return pl.pallas_call(
        functools.partial(_ffn_kernel, final_norm=final_norm),
        grid=(b, s // tm),
        in_specs=[
            tok(D_MODEL), lo, hi, tok(WIDTH),
            _resident((WIDTH, D_MODEL)), _resident((WIDTH, D_MODEL)),
            _resident((1, D_MODEL)),
            _resident((FFN_NCHUNK, D_MODEL, FFN_CHUNK)), _resident((FFN_NCHUNK, D_MODEL, FFN_CHUNK)),
            _resident((FFN_NCHUNK, 3, FFN_CHUNK)), _resident((FFN_NCHUNK, 1, FFN_CHUNK)),
            _resident((FFN_NCHUNK, FFN_CHUNK, D_MODEL)),
            _resident((1, D_MODEL)),
        ],
        out_specs=tok(D_MODEL),
        out_shape=jax.ShapeDtypeStruct((b, s, D_MODEL), F32),
        scratch_shapes=[
            pltpu.VMEM((tm, D_MODEL), F32),
            pltpu.VMEM((tm, D_MODEL), BF16),
            pltpu.VMEM((tm, D_MODEL), F32),
            pltpu.VMEM((FFN_NCHUNK, SUBLANES, FFN_CHUNK), F32),
        ],
        compiler_params=pltpu.CompilerParams(
            dimension_semantics=("arbitrary", "arbitrary"), vmem_limit_bytes=VMEM_LIMIT),
        name="ffn",
    )(x, ya_lo, ya_hi, yr, woa, wor, ln2.reshape(1, D_MODEL), wg, wu, cw, cb, wd,
      ln_f.reshape(1, D_MODEL))


def _chunk_cols(w):
    k = w.shape[0]
    return w.reshape(k, FFN_NCHUNK, FFN_CHUNK).transpose(1, 0, 2)


def kernel(x, ln1, w_in, gn_gain, w_out, ln2, w_up, conv_w, conv_b, w_down, ln_f):
    b, s, d = x.shape
    depth = w_in.shape[0]
    assert d == D_MODEL and s % max(IN_TILE, 2 * FFN_TILE, MOBA_BLOCK, RET_CHUNK) == 0

    inv_a = ROPE_THETA ** (-jnp.arange(ROPE_DIM // 2, dtype=F32) / (ROPE_DIM // 2))
    inv_r = 1.0 / (RET_ROPE_THETA ** jnp.linspace(0.0, 1.0, HEAD_DIM // 2, dtype=F32))
    tabs_a = _rope_tables(s, inv_a, ROPE_DIM // 2)
    tabs_r = _rope_tables(s, inv_r, HEAD_DIM // 2)
    ret_tables = _retention_tables()
    scale = HEAD_DIM ** -0.5
    col_scale = jnp.ones((IN_COLS,), F32).at[0:WIDTH].set(scale).at[4 * WIDTH:5 * WIDTH].set(scale)

    for l in range(depth):
        w_in_l = (w_in[l] * col_scale[None, :]).astype(BF16)
        p = _inproj(x, ln1[l], w_in_l, tabs_a, tabs_r)
        ya_lo, ya_hi = _moba(p)
        yr = _retention(p, gn_gain[l], ret_tables)
        wo = w_out[l].astype(BF16)
        x = _ffn(x, ya_lo, ya_hi, yr, wo[:WIDTH], wo[WIDTH:], ln2[l],
                 _chunk_cols(w_up[l][:, :D_FF]).astype(BF16),
                 _chunk_cols(w_up[l][:, D_FF:]).astype(BF16),
                 _chunk_cols(conv_w[l]), _chunk_cols(conv_b[l][None, :]),
                 w_down[l].astype(BF16).reshape(FFN_NCHUNK, FFN_CHUNK, D_MODEL),
                 ln_f, final_norm=(l == depth - 1))
    return x
```

```python
import functools

import jax
import jax.numpy as jnp
from jax import lax
from jax.experimental import pallas as pl
from jax.experimental.pallas import tpu as pltpu

D_MODEL = 1024
HEAD_DIM = 64
MOBA_HEADS = 8
RET_HEADS = 8
WIDTH = 512
IN_COLS = 7 * WIDTH
MOBA_BLOCK = 256
MOBA_TOPK = 3
ROPE_THETA = 500000.0
ROPE_DIM = HEAD_DIM // 4
RET_ROPE_THETA = 10000.0
RET_CHUNK = 256
D_FF = 2816
NORM_EPS = 1e-6
GN_EPS = 1e-5
NEG_BIG = -1e9

LANES = 128
HEAD_PAIRS = WIDTH // LANES
SUBLANES = 8
VMEM_LIMIT = 56 * 1024 * 1024

IN_TILE = 512
FFN_TILE = 512
FFN_CHUNK = 256
FFN_NCHUNK = D_FF // FFN_CHUNK
RET_STEP_CHUNKS = 4
MOBA_UNROLL = 15
LOG2_E = 1.4426950408889634

F32 = jnp.float32
BF16 = jnp.bfloat16
NT_DIMS = (((1,), (1,)), ((), ()))
TN_DIMS = (((0,), (0,)), ((), ()))


def _rmsnorm(x, g):
    return x * lax.rsqrt(jnp.mean(x * x, axis=-1, keepdims=True) + NORM_EPS) * g


def _resident(shape):
    zeros = (0,) * len(shape)
    return pl.BlockSpec(shape, lambda *_: zeros, pipeline_mode=pl.Buffered(1))


def _rope_tables(seq, inv_freq, half):
    ang = jnp.arange(seq, dtype=F32)[:, None] * inv_freq[None, :]
    cos, sin = jnp.cos(ang), jnp.sin(ang)
    pad = HEAD_DIM - 2 * half
    ones = jnp.ones((seq, pad), F32)
    zeros = jnp.zeros((seq, pad), F32)
    zh = jnp.zeros((seq, half), F32)
    c = jnp.concatenate([cos, cos, ones], axis=1)
    su = jnp.concatenate([-sin, zh, zeros], axis=1)
    sd = jnp.concatenate([zh, sin, zeros], axis=1)
    two = lambda t: jnp.concatenate([t, t], axis=1)
    return two(c), two(su), two(sd)


def _inproj_kernel(x_ref, ln_ref, w_ref, ca_ref, sau_ref, sad_ref, cr_ref, sru_ref, srd_ref,
                   p_ref):
    h = _rmsnorm(x_ref[0], ln_ref[...]).astype(BF16)

    def rope(y, c, su, sd, half):
        outs = []
        for g in range(WIDTH // LANES):
            yg = y[:, g * LANES:(g + 1) * LANES]
            up = pltpu.roll(yg, LANES - half, 1)
            dn = pltpu.roll(yg, half, 1)
            outs.append(yg * c + up * su + dn * sd)
        return jnp.concatenate(outs, axis=1)

    for slab in range(IN_COLS // WIDTH):
        cols = slice(slab * WIDTH, (slab + 1) * WIDTH)
        y = jnp.dot(h, w_ref[:, cols], preferred_element_type=F32)
        if slab in (0, 1):
            y = rope(y, ca_ref[...], sau_ref[...], sad_ref[...], ROPE_DIM // 2)
        elif slab in (3, 4):
            y = rope(y, cr_ref[...], sru_ref[...], srd_ref[...], HEAD_DIM // 2)
        p_ref[0, :, cols] = y.astype(BF16)


def _inproj(x, ln, w_bf16, tabs_a, tabs_r):
    b, s, _ = x.shape
    tm = IN_TILE
    tab = pl.BlockSpec((tm, LANES), lambda si, bi: (si, 0))
    return pl.pallas_call(
        _inproj_kernel,
        grid=(s // tm, b),
        in_specs=[
            pl.BlockSpec((1, tm, D_MODEL), lambda si, bi: (bi, si, 0)),
            _resident((1, D_MODEL)),
            _resident((D_MODEL, IN_COLS)),
            tab, tab, tab, tab, tab, tab,
        ],
        out_specs=pl.BlockSpec((1, tm, IN_COLS), lambda si, bi: (bi, si, 0)),
        out_shape=jax.ShapeDtypeStruct((b, s, IN_COLS), BF16),
        compiler_params=pltpu.CompilerParams(
            dimension_semantics=("arbitrary", "arbitrary"), vmem_limit_bytes=VMEM_LIMIT),
        name="inproj",
    )(x, ln.reshape(1, D_MODEL), w_bf16, *tabs_a, *tabs_r)


def _moba_kernel(qa_ref, qb_ref, k_ref, v_ref, oa_ref, ob_ref, kaug_ref, vaug_ref, kbar_ref,
                 qaug_ref, s_ref, mx_ref, m_ref, acc_ref, *, nblocks):
    p = pl.program_id(2)
    blk = MOBA_BLOCK
    npast = nblocks - 1
    q_blocks = (p, npast - p)
    lane = lax.broadcasted_iota(jnp.int32, (blk, LANES), 1)
    head0 = lane < HEAD_DIM
    lane_b = lane.astype(F32).astype(BF16)
    head0_b = lane_b < HEAD_DIM
    one = jnp.ones((blk, LANES), BF16)
    zero = jnp.zeros((blk, LANES), BF16)

    @pl.when(p == 0)
    def _():
        for n in range(nblocks):
            rows = slice(n * blk, (n + 1) * blk)
            k = k_ref[0, rows, :]
            v = v_ref[0, rows, :]
            kaug_ref[0, rows, :] = jnp.where(head0_b, k, jnp.where(lane_b == HEAD_DIM + n, one, zero))
            kaug_ref[1, rows, :] = jnp.where(head0_b, jnp.where(lane_b == n, one, zero), k)
            vaug_ref[0, rows, :] = jnp.where(head0_b, v, one)
            vaug_ref[1, rows, :] = jnp.where(head0_b, one, v)
            kbar_ref[n:n + 1, :] = jnp.sum(k.astype(F32), axis=0, keepdims=True) / blk

    kbar = kbar_ref[...]
    kb_hi = kbar.astype(BF16)
    kb_lo = (kbar - kb_hi.astype(F32)).astype(BF16)
    kb = jnp.concatenate([kb_hi, kb_lo], axis=0)
    n_iota = lax.broadcasted_iota(jnp.int32, (nblocks, blk), 0)
    fill = jnp.zeros((HEAD_DIM - nblocks, blk), F32)
    for w, q_ref in enumerate((qa_ref, qb_ref)):
        i = q_blocks[w]
        q = q_ref[0]
        past = n_iota < i
        pens = []
        for e in range(2):
            qe = jnp.where(head0_b, q, zero) if e == 0 else jnp.where(head0_b, zero, q)
            g2 = lax.dot_general(kb, qe, NT_DIMS, preferred_element_type=F32)
            g = jnp.where(past, g2[:nblocks] + g2[nblocks:], NEG_BIG)
            cnt = jnp.zeros((nblocks, blk), F32)
            for m in range(nblocks):
                row = g[m:m + 1, :]
                ahead = (row > g) | ((row == g) & (n_iota > m))
                cnt = cnt + jnp.where(ahead, 1.0, 0.0)
            keep = ((cnt < MOBA_TOPK) & past) | (n_iota == i)
            pens.append(jnp.where(keep, 0.0, NEG_BIG))
        pen_t = jnp.concatenate([pens[1], fill, pens[0], fill], axis=0)
        pen = pen_t.T.astype(BF16)
        qaug_ref[w, 0] = jnp.where(head0_b, q, pen)
        qaug_ref[w, 1] = jnp.where(head0_b, pen, q)

    def unit(t):
        w = (t >= p).astype(jnp.int32)
        return w, pl.multiple_of((t - w * p) * blk, blk)

    def scores(w, e, start):
        kblk = kaug_ref[e, pl.ds(start, blk), :]
        return LOG2_E * lax.dot_general(qaug_ref[w, e], kblk, NT_DIMS, preferred_element_type=F32)

    def fold(s):
        return jnp.maximum(s[:, :LANES], s[:, LANES:])

    r_iota = lax.broadcasted_iota(jnp.int32, (blk, blk), 0)
    c_iota = lax.broadcasted_iota(jnp.int32, (blk, blk), 1)
    causal = c_iota <= r_iota
    for w in range(2):
        own = pl.multiple_of(q_blocks[w] * blk, blk)
        for e in range(2):
            s = jnp.where(causal, scores(w, e, own), NEG_BIG)
            s_ref[e, npast + w] = s
            mx_ref[w, e] = fold(s)

    def pass1(tt, carry):
        for u in range(MOBA_UNROLL):
            t = tt * MOBA_UNROLL + u
            w, start = unit(t)
            for e in range(2):
                s = scores(w, e, start)
                s_ref[e, t] = s
                mx_ref[w, e] = jnp.maximum(mx_ref[w, e], fold(s))
        return carry

    lax.fori_loop(0, npast // MOBA_UNROLL, pass1, 0)

    for w in range(2):
        for e in range(2):
            m_ref[w, e] = jnp.broadcast_to(jnp.max(mx_ref[w, e], axis=1, keepdims=True), (blk, LANES))

    def pv(w, e, t, start):
        m = m_ref[w, e]
        prob = jnp.exp2(s_ref[e, t] - jnp.concatenate([m, m], axis=1)).astype(BF16)
        return jnp.dot(prob, vaug_ref[e, pl.ds(start, blk), :], preferred_element_type=F32)

    for w in range(2):
        own = pl.multiple_of(q_blocks[w] * blk, blk)
        for e in range(2):
            acc_ref[w, e] = pv(w, e, npast + w, own)

    def pass2(tt, carry):
        for u in range(MOBA_UNROLL):
            t = tt * MOBA_UNROLL + u
            w, start = unit(t)
            for e in range(2):
                acc_ref[w, e] += pv(w, e, t, start)
        return carry

    lax.fori_loop(0, npast // MOBA_UNROLL, pass2, 0)

    for w, o_ref in enumerate((oa_ref, ob_ref)):
        a0 = acc_ref[w, 0]
        a1 = acc_ref[w, 1]
        out = jnp.where(head0, a0 / pltpu.roll(a0, HEAD_DIM, 1), a1 / pltpu.roll(a1, HEAD_DIM, 1))
        o_ref[0] = out.astype(BF16)


def _moba(p):
    b, s, _ = p.shape
    nblocks = s // MOBA_BLOCK
    nhalf = nblocks // 2
    assert nblocks <= HEAD_DIM and nblocks % SUBLANES == 0 and (nblocks - 1) % MOBA_UNROLL == 0
    blk = MOBA_BLOCK
    kspec = pl.BlockSpec((1, s, LANES), lambda bi, hp, i: (bi, 0, HEAD_PAIRS + hp))
    vspec = pl.BlockSpec((1, s, LANES), lambda bi, hp, i: (bi, 0, 2 * HEAD_PAIRS + hp))
    half = jax.ShapeDtypeStruct((b, s // 2, WIDTH), BF16)
    return pl.pallas_call(
        functools.partial(_moba_kernel, nblocks=nblocks),
        grid=(b, HEAD_PAIRS, nhalf),
        in_specs=[
            pl.BlockSpec((1, blk, LANES), lambda bi, hp, i: (bi, i, hp)),
            pl.BlockSpec((1, blk, LANES), lambda bi, hp, i: (bi, nblocks - 1 - i, hp)),
            kspec, vspec,
        ],
        out_specs=[
            pl.BlockSpec((1, blk, LANES), lambda bi, hp, i: (bi, i, hp)),
            pl.BlockSpec((1, blk, LANES), lambda bi, hp, i: (bi, nhalf - 1 - i, hp)),
        ],
        out_shape=[half, half],
        scratch_shapes=[
            pltpu.VMEM((2, s, LANES), BF16),
            pltpu.VMEM((2, s, LANES), BF16),
            pltpu.VMEM((nblocks, LANES), F32),
            pltpu.VMEM((2, 2, blk, LANES), BF16),
            pltpu.VMEM((2, nblocks + 1, blk, blk), F32),
            pltpu.VMEM((2, 2, blk, LANES), F32),
            pltpu.VMEM((2, 2, blk, LANES), F32),
            pltpu.VMEM((2, 2, blk, LANES), F32),
        ],
        compiler_params=pltpu.CompilerParams(
            dimension_semantics=("arbitrary", "arbitrary", "arbitrary"),
            vmem_limit_bytes=VMEM_LIMIT),
        name="moba",
    )(p, p, p, p)


def _retention_tables():
    c = RET_CHUNK
    log_gamma = jnp.log(1.0 - 2.0 ** (-5.0 - jnp.arange(RET_HEADS, dtype=F32)))
    pos = jnp.arange(c, dtype=F32)
    diff = pos[:, None] - pos[None, :]
    inner = jnp.where(diff[None] >= 0,
                      jnp.exp(jnp.maximum(diff, 0.0)[None] * log_gamma[:, None, None]), 0.0)
    cross = jnp.exp((pos + 1.0)[None, :] * log_gamma[:, None])
    sdec = jnp.exp((c - 1.0 - pos)[None, :] * log_gamma[:, None])
    chunk = jnp.exp(c * log_gamma)

    def lanes(t):
        t = jnp.repeat(t[:, :, None], HEAD_DIM, axis=2)
        t = t.reshape(HEAD_PAIRS, 2, c, HEAD_DIM).transpose(0, 2, 1, 3)
        return t.reshape(HEAD_PAIRS, c, LANES)

    same_head = (jnp.arange(LANES)[:, None] // HEAD_DIM) == (jnp.arange(LANES)[None, :] // HEAD_DIM)
    bd = same_head.astype(F32)
    kdec = jnp.repeat(chunk.reshape(HEAD_PAIRS, 2), HEAD_DIM, axis=1)[:, :, None] * bd[None]
    return inner, lanes(cross), lanes(sdec), kdec, bd


def _retention_kernel(q_ref, k_ref, v_ref, g_ref, idec_ref, cdec_ref, sdec_ref, kdec_ref, bd_ref,
                      gain_ref, o_ref, state_ref):
    lane = lax.broadcasted_iota(jnp.int32, (RET_CHUNK, LANES), 1)
    head0 = lane < HEAD_DIM
    head0_b = lane.astype(F32).astype(BF16) < HEAD_DIM
    zero = jnp.zeros((RET_CHUNK, LANES), BF16)

    @pl.when(pl.program_id(2) == 0)
    def _():
        state_ref[...] = jnp.zeros_like(state_ref)

    def head_mean(t):
        s0 = jnp.sum(jnp.where(head0, t, 0.0), axis=1, keepdims=True)
        s1 = jnp.sum(jnp.where(head0, 0.0, t), axis=1, keepdims=True)
        return jnp.where(head0, s0, s1) / HEAD_DIM

    state = state_ref[...]
    for j in range(RET_STEP_CHUNKS):
        rows = slice(j * RET_CHUNK, (j + 1) * RET_CHUNK)
        q = q_ref[0, rows, :]
        k = k_ref[0, rows, :]
        v = v_ref[0, rows, :]
        o_cross = jnp.dot(q, state.astype(BF16), preferred_element_type=F32) * cdec_ref[0]
        parts = []
        for e in range(2):
            qe = jnp.where(head0_b, q, zero) if e == 0 else jnp.where(head0_b, zero, q)
            s = lax.dot_general(qe, k, NT_DIMS, preferred_element_type=F32) * idec_ref[e]
            parts.append(jnp.dot(s.astype(BF16), v, preferred_element_type=F32))
        o = jnp.where(head0, parts[0], parts[1]) + o_cross

        kd = (k.astype(F32) * sdec_ref[0]).astype(BF16)
        upd = lax.dot_general(kd, v, TN_DIMS, preferred_element_type=F32)
        state = state * kdec_ref[0] + upd * bd_ref[...]

        d = o - head_mean(o)
        on = d * lax.rsqrt(head_mean(d * d) + GN_EPS)
        g = g_ref[0, rows, :].astype(F32)
        y = g * (1.0 / (1.0 + jnp.exp(-g))) * on * gain_ref[...]
        o_ref[0, rows, :] = y.astype(BF16)
    state_ref[...] = state


def _retention(p, gn_gain, tables):
    b, s, _ = p.shape
    c = RET_CHUNK
    inner, cross, sdec, kdec, bd = tables

    rows = RET_STEP_CHUNKS * c

    def col(group):
        return pl.BlockSpec((1, rows, LANES), lambda bi, hp, ci: (bi, ci, group * HEAD_PAIRS + hp))

    pair = lambda shape: pl.BlockSpec(shape, lambda bi, hp, ci: (hp, 0, 0))
    return pl.pallas_call(
        _retention_kernel,
        grid=(b, HEAD_PAIRS, s // rows),
        in_specs=[
            col(3), col(4), col(5), col(6),
            pl.BlockSpec((2, c, c), lambda bi, hp, ci: (hp, 0, 0)),
            pair((1, c, LANES)), pair((1, c, LANES)), pair((1, LANES, LANES)),
            _resident((LANES, LANES)),
            pl.BlockSpec((1, LANES), lambda bi, hp, ci: (0, hp)),
        ],
        out_specs=pl.BlockSpec((1, rows, LANES), lambda bi, hp, ci: (bi, ci, hp)),
        out_shape=jax.ShapeDtypeStruct((b, s, WIDTH), BF16),
        scratch_shapes=[pltpu.VMEM((LANES, LANES), F32)],
        compiler_params=pltpu.CompilerParams(
            dimension_semantics=("arbitrary", "arbitrary", "arbitrary"),
            vmem_limit_bytes=VMEM_LIMIT),
        name="retention",
    )(p, p, p, p, inner, cross, sdec, kdec, bd, gn_gain.reshape(1, WIDTH))


def _ffn_kernel(x_ref, ya_lo_ref, ya_hi_ref, yr_ref, woa_ref, wor_ref, ln2_ref, wg_ref, wu_ref,
                cw_ref, cb_ref, wd_ref, lnf_ref, o_ref, x1_ref, h2_ref, acc_ref, carry_ref,
                *, final_norm):
    tm = x_ref.shape[1]

    @pl.when(pl.program_id(1) == 0)
    def _():
        carry_ref[...] = jnp.zeros_like(carry_ref)

    first_half = pl.program_id(1) < pl.num_programs(1) // 2

    @pl.when(first_half)
    def _():
        x1_ref[...] = jnp.dot(ya_lo_ref[0], woa_ref[...], preferred_element_type=F32)

    @pl.when(jnp.logical_not(first_half))
    def _():
        x1_ref[...] = jnp.dot(ya_hi_ref[0], woa_ref[...], preferred_element_type=F32)

    x1 = x_ref[0] + x1_ref[...] + jnp.dot(yr_ref[0], wor_ref[...], preferred_element_type=F32)
    x1_ref[...] = x1
    h2_ref[...] = _rmsnorm(x1, ln2_ref[...]).astype(BF16)
    rows = lax.broadcasted_iota(jnp.int32, (tm, FFN_CHUNK), 0)
    row0 = rows == 0
    row1 = rows == 1

    def activation(c):
        h2 = h2_ref[...]
        g = jnp.dot(h2, wg_ref[c], preferred_element_type=F32)
        u = jnp.dot(h2, wu_ref[c], preferred_element_type=F32)
        prev = carry_ref[c]
        carry_ref[c] = g[tm - SUBLANES:, :]
        p1 = prev[SUBLANES - 1:SUBLANES, :]
        p2 = prev[SUBLANES - 2:SUBLANES - 1, :]
        g1 = jnp.where(row0, p1, pltpu.roll(g, 1, 0))
        g2 = jnp.where(row0, p2, jnp.where(row1, p1, pltpu.roll(g, 2, 0)))
        cw = cw_ref[c]
        gc = cw[0:1, :] * g2 + cw[1:2, :] * g1 + cw[2:3, :] * g + cb_ref[c]
        return (gc * (1.0 / (1.0 + jnp.exp(-gc))) * u).astype(BF16)

    for c0 in range(0, FFN_NCHUNK, 2):
        part = jnp.dot(activation(c0), wd_ref[c0], preferred_element_type=F32)
        if c0 + 1 < FFN_NCHUNK:
            part += jnp.dot(activation(c0 + 1), wd_ref[c0 + 1], preferred_element_type=F32)
        acc_ref[...] = part if c0 == 0 else acc_ref[...] + part
    x2 = x1_ref[...] + acc_ref[...]
    if final_norm:
        x2 = _rmsnorm(x2, lnf_ref[...])
    o_ref[0] = x2


def _ffn(x, ya_lo, ya_hi, yr, woa, wor, ln2, wg, wu, cw, cb, wd, ln_f, final_norm):
    b, s, _ = x.shape
    tm = FFN_TILE
    nhalf = s // tm // 2
    tok = lambda width: pl.BlockSpec((1, tm, width), lambda bi, ti: (bi, ti, 0))
    lo = pl.BlockSpec((1, tm, WIDTH), lambda bi, ti: (bi, jnp.minimum(ti, nhalf - 1), 0))
    hi = pl.BlockSpec((1, tm, WIDTH), lambda bi, ti: (bi, jnp.maximum(ti - nhalf, 0), 0))
    return pl.pallas_call(
        functools.partial(_ffn_kernel, final_norm=final_norm),
        grid=(b, s // tm),
        in_specs=[
            tok(D_MODEL), lo, hi, tok(WIDTH),
            _resident((WIDTH, D_MODEL)), _resident((WIDTH, D_MODEL)),
            _resident((1, D_MODEL)),
            _resident((FFN_NCHUNK, D_MODEL, FFN_CHUNK)), _resident((FFN_NCHUNK, D_MODEL, FFN_CHUNK)),
            _resident((FFN_NCHUNK, 3, FFN_CHUNK)), _resident((FFN_NCHUNK, 1, FFN_CHUNK)),
            _resident((FFN_NCHUNK, FFN_CHUNK, D_MODEL)),
            _resident((1, D_MODEL)),
        ],
        out_specs=tok(D_MODEL),
        out_shape=jax.ShapeDtypeStruct((b, s, D_MODEL), F32),
        scratch_shapes=[
            pltpu.VMEM((tm, D_MODEL), F32),
            pltpu.VMEM((tm, D_MODEL), BF16),
            pltpu.VMEM((tm, D_MODEL), F32),
            pltpu.VMEM((FFN_NCHUNK, SUBLANES, FFN_CHUNK), F32),
        ],
        compiler_params=pltpu.CompilerParams(
            dimension_semantics=("arbitrary", "arbitrary"), vmem_limit_bytes=VMEM_LIMIT),
        name="ffn",
    )(x, ya_lo, ya_hi, yr, woa, wor, ln2.reshape(1, D_MODEL), wg, wu, cw, cb, wd,
      ln_f.reshape(1, D_MODEL))


def _chunk_cols(w):
    k = w.shape[0]
    return w.reshape(k, FFN_NCHUNK, FFN_CHUNK).transpose(1, 0, 2)


def kernel(x, ln1, w_in, gn_gain, w_out, ln2, w_up, conv_w, conv_b, w_down, ln_f):
    b, s, d = x.shape
    depth = w_in.shape[0]
    assert d == D_MODEL
    assert s % max(IN_TILE, 2 * FFN_TILE, MOBA_BLOCK, RET_STEP_CHUNKS * RET_CHUNK) == 0

    inv_a = ROPE_THETA ** (-jnp.arange(ROPE_DIM // 2, dtype=F32) / (ROPE_DIM // 2))
    inv_r = 1.0 / (RET_ROPE_THETA ** jnp.linspace(0.0, 1.0, HEAD_DIM // 2, dtype=F32))
    tabs_a = _rope_tables(s, inv_a, ROPE_DIM // 2)
    tabs_r = _rope_tables(s, inv_r, HEAD_DIM // 2)
    ret_tables = _retention_tables()
    scale = HEAD_DIM ** -0.5
    col_scale = jnp.ones((IN_COLS,), F32).at[0:WIDTH].set(scale).at[4 * WIDTH:5 * WIDTH].set(scale)

    for l in range(depth):
        w_in_l = (w_in[l] * col_scale[None, :]).astype(BF16)
        p = _inproj(x, ln1[l], w_in_l, tabs_a, tabs_r)
        ya_lo, ya_hi = _moba(p)
        yr = _retention(p, gn_gain[l], ret_tables)
        wo = w_out[l].astype(BF16)
        x = _ffn(x, ya_lo, ya_hi, yr, wo[:WIDTH], wo[WIDTH:], ln2[l],
                 _chunk_cols(w_up[l][:, :D_FF]).astype(BF16),
                 _chunk_cols(w_up[l][:, D_FF:]).astype(BF16),
                 _chunk_cols(conv_w[l]), _chunk_cols(conv_b[l][None, :]),
                 w_down[l].astype(BF16).reshape(FFN_NCHUNK, FFN_CHUNK, D_MODEL),
                 ln_f, final_norm=(l == depth - 1))
    return x
```

```python
import functools

import jax
import jax.numpy as jnp
from jax import lax
from jax.experimental import pallas as pl
from jax.experimental.pallas import tpu as pltpu

D_MODEL = 1024
HEAD_DIM = 64
MOBA_HEADS = 8
RET_HEADS = 8
WIDTH = 512
IN_COLS = 7 * WIDTH
MOBA_BLOCK = 256
MOBA_TOPK = 3
ROPE_THETA = 500000.0
ROPE_DIM = HEAD_DIM // 4
RET_ROPE_THETA = 10000.0
RET_CHUNK = 256
D_FF = 2816
NORM_EPS = 1e-6
GN_EPS = 1e-5
NEG_BIG = -1e9

LANES = 128
HEAD_PAIRS = WIDTH // LANES
SUBLANES = 8
VMEM_LIMIT = 56 * 1024 * 1024

IN_TILE = 512
FFN_TILE = 512
FFN_CHUNK = 256
FFN_NCHUNK = D_FF // FFN_CHUNK
FFN_SLOTS = 6
RET_STEP_CHUNKS = 4
MOBA_PAIRS = 2
LOG2_E = 1.4426950408889634

F32 = jnp.float32
BF16 = jnp.bfloat16
NT_DIMS = (((1,), (1,)), ((), ()))
TN_DIMS = (((0,), (0,)), ((), ()))


def _rmsnorm(x, g):
    return x * lax.rsqrt(jnp.mean(x * x, axis=-1, keepdims=True) + NORM_EPS) * g


def _resident(shape):
    zeros = (0,) * len(shape)
    return pl.BlockSpec(shape, lambda *_: zeros, pipeline_mode=pl.Buffered(1))


def _rope_tables(seq, inv_freq, half):
    ang = jnp.arange(seq, dtype=F32)[:, None] * inv_freq[None, :]
    cos, sin = jnp.cos(ang), jnp.sin(ang)
    pad = HEAD_DIM - 2 * half
    ones = jnp.ones((seq, pad), F32)
    zeros = jnp.zeros((seq, pad), F32)
    zh = jnp.zeros((seq, half), F32)
    c = jnp.concatenate([cos, cos, ones], axis=1)
    su = jnp.concatenate([-sin, zh, zeros], axis=1)
    sd = jnp.concatenate([zh, sin, zeros], axis=1)
    two = lambda t: jnp.concatenate([t, t], axis=1)
    return two(c), two(su), two(sd)


def _inproj_kernel(x_ref, ln_ref, w_ref, ca_ref, sau_ref, sad_ref, cr_ref, sru_ref, srd_ref,
                   p_ref):
    h = _rmsnorm(x_ref[0], ln_ref[...]).astype(BF16)

    def rope(y, c, su, sd, half):
        outs = []
        for g in range(WIDTH // LANES):
            yg = y[:, g * LANES:(g + 1) * LANES]
            up = pltpu.roll(yg, LANES - half, 1)
            dn = pltpu.roll(yg, half, 1)
            outs.append(yg * c + up * su + dn * sd)
        return jnp.concatenate(outs, axis=1)

    for slab in range(IN_COLS // WIDTH):
        cols = slice(slab * WIDTH, (slab + 1) * WIDTH)
        y = jnp.dot(h, w_ref[:, cols], preferred_element_type=F32)
        if slab in (0, 1):
            y = rope(y, ca_ref[...], sau_ref[...], sad_ref[...], ROPE_DIM // 2)
        elif slab in (3, 4):
            y = rope(y, cr_ref[...], sru_ref[...], srd_ref[...], HEAD_DIM // 2)
        p_ref[0, :, cols] = y.astype(BF16)


def _inproj(x, ln, w_bf16, tabs_a, tabs_r):
    b, s, _ = x.shape
    tm = IN_TILE
    tab = pl.BlockSpec((tm, LANES), lambda si, bi: (si, 0))
    return pl.pallas_call(
        _inproj_kernel,
        grid=(s // tm, b),
        in_specs=[
            pl.BlockSpec((1, tm, D_MODEL), lambda si, bi: (bi, si, 0)),
            _resident((1, D_MODEL)),
            _resident((D_MODEL, IN_COLS)),
            tab, tab, tab, tab, tab, tab,
        ],
        out_specs=pl.BlockSpec((1, tm, IN_COLS), lambda si, bi: (bi, si, 0)),
        out_shape=jax.ShapeDtypeStruct((b, s, IN_COLS), BF16),
        compiler_params=pltpu.CompilerParams(
            dimension_semantics=("arbitrary", "arbitrary"), vmem_limit_bytes=VMEM_LIMIT),
        name="inproj",
    )(x, ln.reshape(1, D_MODEL), w_bf16, *tabs_a, *tabs_r)


def _moba_kernel(qlo_ref, qhi_ref, qlon_ref, qhin_ref, k_ref, v_ref, olo_ref, ohi_ref,
                 kaug_ref, vaug_ref, kbar_ref, qaug_ref, qnext_ref, s_ref, mx_ref, m_ref, acc_ref,
                 *, nblocks):
    u = pl.program_id(2)
    blk = MOBA_BLOCK
    npast = nblocks - 1
    lane = lax.broadcasted_iota(jnp.int32, (blk, LANES), 1)
    head0 = lane < HEAD_DIM
    lane_b = lane.astype(F32).astype(BF16)
    head0_b = lane_b < HEAD_DIM
    one = jnp.ones((blk, LANES), BF16)
    zero = jnp.zeros((blk, LANES), BF16)

    def build_queries(lo_ref, hi_ref, step):
        kbar = kbar_ref[...]
        kb_hi = kbar.astype(BF16)
        kb_lo = (kbar - kb_hi.astype(F32)).astype(BF16)
        kb = jnp.concatenate([kb_hi, kb_lo], axis=0)
        n_iota = lax.broadcasted_iota(jnp.int32, (nblocks, blk), 0)
        fill = jnp.zeros((HEAD_DIM - nblocks, blk), F32)
        for ps in range(MOBA_PAIRS):
            first = MOBA_PAIRS * step + ps
            sources = ((lo_ref, ps, first), (hi_ref, MOBA_PAIRS - 1 - ps, npast - first))
            for w, (q_ref, half, i) in enumerate(sources):
                q = q_ref[0, half * blk:(half + 1) * blk, :]
                past = n_iota < i
                pens = []
                for e in range(2):
                    qe = jnp.where(head0_b, q, zero) if e == 0 else jnp.where(head0_b, zero, q)
                    g2 = lax.dot_general(kb, qe, NT_DIMS, preferred_element_type=F32)
                    g = jnp.where(past, g2[:nblocks] + g2[nblocks:], NEG_BIG)
                    cnt = jnp.zeros((nblocks, blk), F32)
                    for m in range(nblocks):
                        row = g[m:m + 1, :]
                        ahead = (row > g) | ((row == g) & (n_iota > m))
                        cnt = cnt + jnp.where(ahead, 1.0, 0.0)
                    keep = ((cnt < MOBA_TOPK) & past) | (n_iota == i)
                    pens.append(jnp.where(keep, 0.0, NEG_BIG))
                pen_t = jnp.concatenate([pens[1], fill, pens[0], fill], axis=0)
                pen = pen_t.T.astype(BF16)
                qnext_ref[ps, w, 0] = jnp.where(head0_b, q, pen)
                qnext_ref[ps, w, 1] = jnp.where(head0_b, pen, q)

    @pl.when(u == 0)
    def _():
        for n in range(nblocks):
            rows = slice(n * blk, (n + 1) * blk)
            k = k_ref[0, rows, :]
            v = v_ref[0, rows, :]
            kaug_ref[0, rows, :] = jnp.where(head0_b, k, jnp.where(lane_b == HEAD_DIM + n, one, zero))
            kaug_ref[1, rows, :] = jnp.where(head0_b, jnp.where(lane_b == n, one, zero), k)
            vaug_ref[0, rows, :] = jnp.where(head0_b, v, one)
            vaug_ref[1, rows, :] = jnp.where(head0_b, one, v)
            kbar_ref[n:n + 1, :] = jnp.sum(k.astype(F32), axis=0, keepdims=True) / blk
        build_queries(qlo_ref, qhi_ref, 0)

    qaug_ref[...] = qnext_ref[...]
    build_queries(qlon_ref, qhin_ref, jnp.minimum(u + 1, pl.num_programs(2) - 1))

    r_iota = lax.broadcasted_iota(jnp.int32, (blk, blk), 0)
    c_iota = lax.broadcasted_iota(jnp.int32, (blk, blk), 1)
    causal = c_iota <= r_iota

    def unit(first, t):
        w = (t >= first).astype(jnp.int32)
        return w, pl.multiple_of((t - w * first) * blk, blk)

    def scores(ps, w, e, start):
        kblk = kaug_ref[e, pl.ds(start, blk), :]
        return LOG2_E * lax.dot_general(qaug_ref[ps, w, e], kblk, NT_DIMS,
                                        preferred_element_type=F32)

    def fold(s):
        return jnp.maximum(s[:, :LANES], s[:, LANES:])

    def own_start(first, w):
        return pl.multiple_of((first, npast - first)[w] * blk, blk)

    def pass1(ps, first):
        for w in range(2):
            for e in range(2):
                s = jnp.where(causal, scores(ps, w, e, own_start(first, w)), NEG_BIG)
                s_ref[ps, e, npast + w] = s
                mx_ref[ps, w, e] = fold(s)
        for t in range(npast):
            w, start = unit(first, t)
            for e in range(2):
                s = scores(ps, w, e, start)
                s_ref[ps, e, t] = s
                mx_ref[ps, w, e] = jnp.maximum(mx_ref[ps, w, e], fold(s))

    def pv(ps, w, e, t, start):
        m = m_ref[ps, w, e]
        prob = jnp.exp2(s_ref[ps, e, t] - jnp.concatenate([m, m], axis=1)).astype(BF16)
        return jnp.dot(prob, vaug_ref[e, pl.ds(start, blk), :], preferred_element_type=F32)

    def pass2(ps, first):
        for w in range(2):
            for e in range(2):
                row_max = jnp.max(mx_ref[ps, w, e], axis=1, keepdims=True)
                m_ref[ps, w, e] = jnp.broadcast_to(row_max, (blk, LANES))
        for w in range(2):
            for e in range(2):
                acc_ref[ps, w, e] = pv(ps, w, e, npast + w, own_start(first, w))
        for t in range(npast):
            w, start = unit(first, t)
            for e in range(2):
                acc_ref[ps, w, e] += pv(ps, w, e, t, start)

    def finish(ps):
        targets = ((olo_ref, ps), (ohi_ref, MOBA_PAIRS - 1 - ps))
        for w, (o_ref, half) in enumerate(targets):
            a0 = acc_ref[ps, w, 0]
            a1 = acc_ref[ps, w, 1]
            out = jnp.where(head0, a0 / pltpu.roll(a0, HEAD_DIM, 1), a1 / pltpu.roll(a1, HEAD_DIM, 1))
            o_ref[0, half * blk:(half + 1) * blk, :] = out.astype(BF16)

    firsts = [MOBA_PAIRS * u + ps for ps in range(MOBA_PAIRS)]
    for ps in range(MOBA_PAIRS):
        pass1(ps, firsts[ps])
    for ps in range(MOBA_PAIRS):
        pass2(ps, firsts[ps])
    for ps in range(MOBA_PAIRS):
        finish(ps)


def _moba(p):
    b, s, _ = p.shape
    nblocks = s // MOBA_BLOCK
    blk = MOBA_BLOCK
    rows = MOBA_PAIRS * blk
    steps = nblocks // 2 // MOBA_PAIRS
    assert MOBA_PAIRS == 2 and nblocks <= HEAD_DIM and nblocks % (2 * MOBA_PAIRS) == 0
    assert nblocks % SUBLANES == 0
    nxt = lambda i: jnp.minimum(i + 1, steps - 1)
    qspec = lambda row_block: pl.BlockSpec(
        (1, rows, LANES), lambda bi, hp, i: (bi, row_block(i), hp))
    kspec = pl.BlockSpec((1, s, LANES), lambda bi, hp, i: (bi, 0, HEAD_PAIRS + hp))
    vspec = pl.BlockSpec((1, s, LANES), lambda bi, hp, i: (bi, 0, 2 * HEAD_PAIRS + hp))
    half = jax.ShapeDtypeStruct((b, s // 2, WIDTH), BF16)
    last = 2 * steps - 1
    return pl.pallas_call(
        functools.partial(_moba_kernel, nblocks=nblocks),
        grid=(b, HEAD_PAIRS, steps),
        in_specs=[
            qspec(lambda i: i), qspec(lambda i: last - i),
            qspec(nxt), qspec(lambda i: last - nxt(i)),
            kspec, vspec,
        ],
        out_specs=[
            pl.BlockSpec((1, rows, LANES), lambda bi, hp, i: (bi, i, hp)),
            pl.BlockSpec((1, rows, LANES), lambda bi, hp, i: (bi, steps - 1 - i, hp)),
        ],
        out_shape=[half, half],
        scratch_shapes=[
            pltpu.VMEM((2, s, LANES), BF16),
            pltpu.VMEM((2, s, LANES), BF16),
            pltpu.VMEM((nblocks, LANES), F32),
            pltpu.VMEM((MOBA_PAIRS, 2, 2, blk, LANES), BF16),
            pltpu.VMEM((MOBA_PAIRS, 2, 2, blk, LANES), BF16),
            pltpu.VMEM((MOBA_PAIRS, 2, nblocks + 1, blk, blk), F32),
            pltpu.VMEM((MOBA_PAIRS, 2, 2, blk, LANES), F32),
            pltpu.VMEM((MOBA_PAIRS, 2, 2, blk, LANES), F32),
            pltpu.VMEM((MOBA_PAIRS, 2, 2, blk, LANES), F32),
        ],
        compiler_params=pltpu.CompilerParams(
            dimension_semantics=("arbitrary", "arbitrary", "arbitrary"),
            vmem_limit_bytes=VMEM_LIMIT),
        name="moba",
    )(p, p, p, p, p, p)


def _retention_tables():
    c = RET_CHUNK
    log_gamma = jnp.log(1.0 - 2.0 ** (-5.0 - jnp.arange(RET_HEADS, dtype=F32)))
    pos = jnp.arange(c, dtype=F32)
    diff = pos[:, None] - pos[None, :]
    inner = jnp.where(diff[None] >= 0,
                      jnp.exp(jnp.maximum(diff, 0.0)[None] * log_gamma[:, None, None]), 0.0)
    cross = jnp.exp((pos + 1.0)[None, :] * log_gamma[:, None])
    sdec = jnp.exp((c - 1.0 - pos)[None, :] * log_gamma[:, None])
    chunk = jnp.exp(c * log_gamma)

    def lanes(t):
        t = jnp.repeat(t[:, :, None], HEAD_DIM, axis=2)
        t = t.reshape(HEAD_PAIRS, 2, c, HEAD_DIM).transpose(0, 2, 1, 3)
        return t.reshape(HEAD_PAIRS, c, LANES)

    same_head = (jnp.arange(LANES)[:, None] // HEAD_DIM) == (jnp.arange(LANES)[None, :] // HEAD_DIM)
    bd = same_head.astype(F32)
    kdec = jnp.repeat(chunk.reshape(HEAD_PAIRS, 2), HEAD_DIM, axis=1)[:, :, None] * bd[None]
    return inner, lanes(cross), lanes(sdec), kdec, bd


def _retention_kernel(q_ref, k_ref, v_ref, g_ref, idec_ref, cdec_ref, sdec_ref, kdec_ref, bd_ref,
                      gain_ref, o_ref, state_ref):
    lane = lax.broadcasted_iota(jnp.int32, (RET_CHUNK, LANES), 1)
    head0 = lane < HEAD_DIM
    head0_b = lane.astype(F32).astype(BF16) < HEAD_DIM
    zero = jnp.zeros((RET_CHUNK, LANES), BF16)

    @pl.when(pl.program_id(2) == 0)
    def _():
        state_ref[...] = jnp.zeros_like(state_ref)

    def head_mean(t):
        s0 = jnp.sum(jnp.where(head0, t, 0.0), axis=1, keepdims=True)
        s1 = jnp.sum(jnp.where(head0, 0.0, t), axis=1, keepdims=True)
        return jnp.where(head0, s0, s1) / HEAD_DIM

    state = state_ref[...]
    for j in range(RET_STEP_CHUNKS):
        rows = slice(j * RET_CHUNK, (j + 1) * RET_CHUNK)
        q = q_ref[0, rows, :]
        k = k_ref[0, rows, :]
        v = v_ref[0, rows, :]
        o_cross = jnp.dot(q, state.astype(BF16), preferred_element_type=F32) * cdec_ref[0]
        parts = []
        for e in range(2):
            qe = jnp.where(head0_b, q, zero) if e == 0 else jnp.where(head0_b, zero, q)
            s = lax.dot_general(qe, k, NT_DIMS, preferred_element_type=F32) * idec_ref[e]
            parts.append(jnp.dot(s.astype(BF16), v, preferred_element_type=F32))
        o = jnp.where(head0, parts[0], parts[1]) + o_cross

        kd = (k.astype(F32) * sdec_ref[0]).astype(BF16)
        upd = lax.dot_general(kd, v, TN_DIMS, preferred_element_type=F32)
        state = state * kdec_ref[0] + upd * bd_ref[...]

        d = o - head_mean(o)
        on = d * lax.rsqrt(head_mean(d * d) + GN_EPS)
        g = g_ref[0, rows, :].astype(F32)
        y = g * (1.0 / (1.0 + jnp.exp(-g))) * on * gain_ref[...]
        o_ref[0, rows, :] = y.astype(BF16)
    state_ref[...] = state


def _retention(p, gn_gain, tables):
    b, s, _ = p.shape
    c = RET_CHUNK
    inner, cross, sdec, kdec, bd = tables

    rows = RET_STEP_CHUNKS * c

    def col(group):
        return pl.BlockSpec((1, rows, LANES), lambda bi, hp, ci: (bi, ci, group * HEAD_PAIRS + hp))

    pair = lambda shape: pl.BlockSpec(shape, lambda bi, hp, ci: (hp, 0, 0))
    return pl.pallas_call(
        _retention_kernel,
        grid=(b, HEAD_PAIRS, s // rows),
        in_specs=[
            col(3), col(4), col(5), col(6),
            pl.BlockSpec((2, c, c), lambda bi, hp, ci: (hp, 0, 0)),
            pair((1, c, LANES)), pair((1, c, LANES)), pair((1, LANES, LANES)),
            _resident((LANES, LANES)),
            pl.BlockSpec((1, LANES), lambda bi, hp, ci: (0, hp)),
        ],
        out_specs=pl.BlockSpec((1, rows, LANES), lambda bi, hp, ci: (bi, ci, hp)),
        out_shape=jax.ShapeDtypeStruct((b, s, WIDTH), BF16),
        scratch_shapes=[pltpu.VMEM((LANES, LANES), F32)],
        compiler_params=pltpu.CompilerParams(
            dimension_semantics=("arbitrary", "arbitrary", "arbitrary"),
            vmem_limit_bytes=VMEM_LIMIT),
        name="retention",
    )(p, p, p, p, inner, cross, sdec, kdec, bd, gn_gain.reshape(1, WIDTH))


def _ffn_kernel(x_ref, ya_lo_ref, ya_hi_ref, yr_ref, woa_ref, wor_ref, ln2_ref, wg_ref, wu_ref,
                cw_ref, cb_ref, wd_ref, lnf_ref, o_ref, x1_ref, h2_ref, acc_ref, carry_ref, gu_ref,
                *, final_norm):
    tm = x_ref.shape[1]

    @pl.when(pl.program_id(1) == 0)
    def _():
        carry_ref[...] = jnp.zeros_like(carry_ref)

    first_half = pl.program_id(1) < pl.num_programs(1) // 2

    @pl.when(first_half)
    def _():
        x1_ref[...] = jnp.dot(ya_lo_ref[0], woa_ref[...], preferred_element_type=F32)

    @pl.when(jnp.logical_not(first_half))
    def _():
        x1_ref[...] = jnp.dot(ya_hi_ref[0], woa_ref[...], preferred_element_type=F32)

    x1 = x_ref[0] + x1_ref[...] + jnp.dot(yr_ref[0], wor_ref[...], preferred_element_type=F32)
    x1_ref[...] = x1
    h2_ref[...] = _rmsnorm(x1, ln2_ref[...]).astype(BF16)
    rows = lax.broadcasted_iota(jnp.int32, (tm, FFN_CHUNK), 0)
    row0 = rows == 0
    row1 = rows == 1

    def up_proj(c):
        h2 = h2_ref[...]
        gu_ref[c % FFN_SLOTS, 0] = jnp.dot(h2, wg_ref[c], preferred_element_type=F32)
        gu_ref[c % FFN_SLOTS, 1] = jnp.dot(h2, wu_ref[c], preferred_element_type=F32)

    def activation(c):
        g = gu_ref[c % FFN_SLOTS, 0]
        u = gu_ref[c % FFN_SLOTS, 1]
        prev = carry_ref[c]
        carry_ref[c] = g[tm - SUBLANES:, :]
        p1 = prev[SUBLANES - 1:SUBLANES, :]
        p2 = prev[SUBLANES - 2:SUBLANES - 1, :]
        g1 = jnp.where(row0, p1, pltpu.roll(g, 1, 0))
        g2 = jnp.where(row0, p2, jnp.where(row1, p1, pltpu.roll(g, 2, 0)))
        cw = cw_ref[c]
        gc = cw[0:1, :] * g2 + cw[1:2, :] * g1 + cw[2:3, :] * g + cb_ref[c]
        return (gc * (1.0 / (1.0 + jnp.exp(-gc))) * u).astype(BF16)

    for c in range(min(FFN_SLOTS, FFN_NCHUNK)):
        up_proj(c)
    for c0 in range(0, FFN_NCHUNK, 2):
        part = None
        for c in range(c0, min(c0 + 2, FFN_NCHUNK)):
            act = activation(c)
            if c + FFN_SLOTS < FFN_NCHUNK:
                up_proj(c + FFN_SLOTS)
            down = jnp.dot(act, wd_ref[c], preferred_element_type=F32)
            part = down if part is None else part + down
        acc_ref[...] = part if c0 == 0 else acc_ref[...] + part
    x2 = x1_ref[...] + acc_ref[...]
    if final_norm:
        x2 = _rmsnorm(x2, lnf_ref[...])
    o_ref[0] = x2


def _ffn(x, ya_lo, ya_hi, yr, woa, wor, ln2, wg, wu, cw, cb, wd, ln_f, final_norm):
    b, s, _ = x.shape
    tm = FFN_TILE
    nhalf = s // tm // 2
    tok = lambda width: pl.BlockSpec((1, tm, width), lambda bi, ti: (bi, ti, 0))
    lo = pl.BlockSpec((1, tm, WIDTH), lambda bi, ti: (bi, jnp.minimum(ti, nhalf - 1), 0))
    hi = pl.BlockSpec((1, tm, WIDTH), lambda bi, ti: (bi, jnp.maximum(ti - nhalf, 0), 0))
    return pl.pallas_call(
        functools.partial(_ffn_kernel, final_norm=final_norm),
        grid=(b, s // tm),
        in_specs=[
            tok(D_MODEL), lo, hi, tok(WIDTH),
            _resident((WIDTH, D_MODEL)), _resident((WIDTH, D_MODEL)),
            _resident((1, D_MODEL)),
            _resident((FFN_NCHUNK, D_MODEL, FFN_CHUNK)), _resident((FFN_NCHUNK, D_MODEL, FFN_CHUNK)),
            _resident((FFN_NCHUNK, 3, FFN_CHUNK)), _resident((FFN_NCHUNK, 1, FFN_CHUNK)),
            _resident((FFN_NCHUNK, FFN_CHUNK, D_MODEL)),
            _resident((1, D_MODEL)),
        ],
        out_specs=tok(D_MODEL),
        out_shape=jax.ShapeDtypeStruct((b, s, D_MODEL), F32),
        scratch_shapes=[
            pltpu.VMEM((tm, D_MODEL), F32),
            pltpu.VMEM((tm, D_MODEL), BF16),
            pltpu.VMEM((tm, D_MODEL), F32),
            pltpu.VMEM((FFN_NCHUNK, SUBLANES, FFN_CHUNK), F32),
            pltpu.VMEM((FFN_SLOTS, 2, tm, FFN_CHUNK), F32),
        ],
        compiler_params=pltpu.CompilerParams(
            dimension_semantics=("arbitrary", "arbitrary"), vmem_limit_bytes=VMEM_LIMIT),
        name="ffn",
    )(x, ya_lo, ya_hi, yr, woa, wor, ln2.reshape(1, D_MODEL), wg, wu, cw, cb, wd,
      ln_f.reshape(1, D_MODEL))


def _chunk_cols(w):
    k = w.shape[0]
    return w.reshape(k, FFN_NCHUNK, FFN_CHUNK).transpose(1, 0, 2)


def kernel(x, ln1, w_in, gn_gain, w_out, ln2, w_up, conv_w, conv_b, w_down, ln_f):
    b, s, d = x.shape
    depth = w_in.shape[0]
    assert d == D_MODEL
    assert s % max(IN_TILE, 2 * FFN_TILE, MOBA_BLOCK, RET_STEP_CHUNKS * RET_CHUNK) == 0

    inv_a = ROPE_THETA ** (-jnp.arange(ROPE_DIM // 2, dtype=F32) / (ROPE_DIM // 2))
    inv_r = 1.0 / (RET_ROPE_THETA ** jnp.linspace(0.0, 1.0, HEAD_DIM // 2, dtype=F32))
    tabs_a = _rope_tables(s, inv_a, ROPE_DIM // 2)
    tabs_r = _rope_tables(s, inv_r, HEAD_DIM // 2)
    ret_tables = _retention_tables()
    scale = HEAD_DIM ** -0.5
    col_scale = jnp.ones((IN_COLS,), F32).at[0:WIDTH].set(scale).at[4 * WIDTH:5 * WIDTH].set(scale)

    for l in range(depth):
        w_in_l = (w_in[l] * col_scale[None, :]).astype(BF16)
        p = _inproj(x, ln1[l], w_in_l, tabs_a, tabs_r)
        ya_lo, ya_hi = _moba(p)
        yr = _retention(p, gn_gain[l], ret_tables)
        wo = w_out[l].astype(BF16)
        x = _ffn(x, ya_lo, ya_hi, yr, wo[:WIDTH], wo[WIDTH:], ln2[l],
                 _chunk_cols(w_up[l][:, :D_FF]).astype(BF16),
                 _chunk_cols(w_up[l][:, D_FF:]).astype(BF16),
                 _chunk_cols(conv_w[l]), _chunk_cols(conv_b[l][None, :]),
                 w_down[l].astype(BF16).reshape(FFN_NCHUNK, FFN_CHUNK, D_MODEL),
                 ln_f, final_norm=(l == depth - 1))
    return x
```

```python
import functools

import jax
import jax.numpy as jnp
from jax import lax
from jax.experimental import pallas as pl
from jax.experimental.pallas import tpu as pltpu

D_MODEL = 1024
HEAD_DIM = 64
MOBA_HEADS = 8
RET_HEADS = 8
WIDTH = 512
IN_COLS = 7 * WIDTH
MOBA_BLOCK = 256
MOBA_TOPK = 3
ROPE_THETA = 500000.0
ROPE_DIM = HEAD_DIM // 4
RET_ROPE_THETA = 10000.0
RET_CHUNK = 256
D_FF = 2816
NORM_EPS = 1e-6
GN_EPS = 1e-5
NEG_BIG = -1e9

LANES = 128
HEAD_PAIRS = WIDTH // LANES
SUBLANES = 8
VMEM_LIMIT = 56 * 1024 * 1024

IN_TILE = 512
FFN_TILE = 512
FFN_CHUNK = 256
FFN_NCHUNK = D_FF // FFN_CHUNK
FFN_SLOTS = 6
RET_STEP_CHUNKS = 4
MOBA_PAIRS = 2
LOG2_E = 1.4426950408889634

F32 = jnp.float32
BF16 = jnp.bfloat16
NT_DIMS = (((1,), (1,)), ((), ()))
TN_DIMS = (((0,), (0,)), ((), ()))


def _rmsnorm(x, g):
    return x * lax.rsqrt(jnp.mean(x * x, axis=-1, keepdims=True) + NORM_EPS) * g


def _resident(shape):
    zeros = (0,) * len(shape)
    return pl.BlockSpec(shape, lambda *_: zeros, pipeline_mode=pl.Buffered(1))


def _rope_tables(seq, inv_freq, half):
    ang = jnp.arange(seq, dtype=F32)[:, None] * inv_freq[None, :]
    cos, sin = jnp.cos(ang), jnp.sin(ang)
    pad = HEAD_DIM - 2 * half
    ones = jnp.ones((seq, pad), F32)
    zeros = jnp.zeros((seq, pad), F32)
    zh = jnp.zeros((seq, half), F32)
    c = jnp.concatenate([cos, cos, ones], axis=1)
    su = jnp.concatenate([-sin, zh, zeros], axis=1)
    sd = jnp.concatenate([zh, sin, zeros], axis=1)
    two = lambda t: jnp.concatenate([t, t], axis=1)
    return two(c), two(su), two(sd)


def _inproj_kernel(x_ref, ln_ref, w_ref, ca_ref, sau_ref, sad_ref, cr_ref, sru_ref, srd_ref,
                   p_ref):
    h = _rmsnorm(x_ref[0], ln_ref[...]).astype(BF16)

    def rope(y, c, su, sd, half):
        outs = []
        for g in range(WIDTH // LANES):
            yg = y[:, g * LANES:(g + 1) * LANES]
            up = pltpu.roll(yg, LANES - half, 1)
            dn = pltpu.roll(yg, half, 1)
            outs.append(yg * c + up * su + dn * sd)
        return jnp.concatenate(outs, axis=1)

    for slab in range(IN_COLS // WIDTH):
        cols = slice(slab * WIDTH, (slab + 1) * WIDTH)
        y = jnp.dot(h, w_ref[:, cols], preferred_element_type=F32)
        if slab in (0, 1):
            y = rope(y, ca_ref[...], sau_ref[...], sad_ref[...], ROPE_DIM // 2)
        elif slab in (3, 4):
            y = rope(y, cr_ref[...], sru_ref[...], srd_ref[...], HEAD_DIM // 2)
        p_ref[0, :, cols] = y.astype(BF16)


def _inproj(x, ln, w_bf16, tabs_a, tabs_r):
    b, s, _ = x.shape
    tm = IN_TILE
    tab = pl.BlockSpec((tm, LANES), lambda si, bi: (si, 0))
    return pl.pallas_call(
        _inproj_kernel,
        grid=(s // tm, b),
        in_specs=[
            pl.BlockSpec((1, tm, D_MODEL), lambda si, bi: (bi, si, 0)),
            _resident((1, D_MODEL)),
            _resident((D_MODEL, IN_COLS)),
            tab, tab, tab, tab, tab, tab,
        ],
        out_specs=pl.BlockSpec((1, tm, IN_COLS), lambda si, bi: (bi, si, 0)),
        out_shape=jax.ShapeDtypeStruct((b, s, IN_COLS), BF16),
        compiler_params=pltpu.CompilerParams(
            dimension_semantics=("arbitrary", "arbitrary"), vmem_limit_bytes=VMEM_LIMIT),
        name="inproj",
    )(x, ln.reshape(1, D_MODEL), w_bf16, *tabs_a, *tabs_r)


def _moba_kernel(qlo_ref, qhi_ref, qlon_ref, qhin_ref, k_ref, v_ref, olo_ref, ohi_ref,
                 kaug_ref, vaug_ref, kbar_ref, qaug_ref, qnext_ref, s_ref, mx_ref, m_ref, acc_ref,
                 *, nblocks):
    u = pl.program_id(2)
    blk = MOBA_BLOCK
    npast = nblocks - 1
    lane = lax.broadcasted_iota(jnp.int32, (blk, LANES), 1)
    head0 = lane < HEAD_DIM
    lane_b = lane.astype(F32).astype(BF16)
    head0_b = lane_b < HEAD_DIM
    one = jnp.ones((blk, LANES), BF16)
    zero = jnp.zeros((blk, LANES), BF16)

    def build_queries(lo_ref, hi_ref, step):
        kbar = kbar_ref[...]
        kb_hi = kbar.astype(BF16)
        kb_lo = (kbar - kb_hi.astype(F32)).astype(BF16)
        kb = jnp.concatenate([kb_hi, kb_lo], axis=0)
        n_iota = lax.broadcasted_iota(jnp.int32, (nblocks, blk), 0)
        fill = jnp.zeros((HEAD_DIM - nblocks, blk), F32)
        for ps in range(MOBA_PAIRS):
            first = MOBA_PAIRS * step + ps
            sources = ((lo_ref, ps, first), (hi_ref, MOBA_PAIRS - 1 - ps, npast - first))
            for w, (q_ref, half, i) in enumerate(sources):
                q = q_ref[0, half * blk:(half + 1) * blk, :]
                past = n_iota < i
                pens = []
                for e in range(2):
                    qe = jnp.where(head0_b, q, zero) if e == 0 else jnp.where(head0_b, zero, q)
                    g2 = lax.dot_general(kb, qe, NT_DIMS, preferred_element_type=F32)
                    g = jnp.where(past, g2[:nblocks] + g2[nblocks:], NEG_BIG)
                    cnt = jnp.zeros((nblocks, blk), F32)
                    for m in range(nblocks):
                        row = g[m:m + 1, :]
                        ahead = (row > g) | ((row == g) & (n_iota > m))
                        cnt = cnt + jnp.where(ahead, 1.0, 0.0)
                    keep = ((cnt < MOBA_TOPK) & past) | (n_iota == i)
                    pens.append(jnp.where(keep, 0.0, NEG_BIG))
                pen_t = jnp.concatenate([pens[1], fill, pens[0], fill], axis=0)
                pen = pen_t.T.astype(BF16)
                qnext_ref[ps, w, 0] = jnp.where(head0_b, q, pen)
                qnext_ref[ps, w, 1] = jnp.where(head0_b, pen, q)

    @pl.when(u == 0)
    def _():
        for n in range(nblocks):
            rows = slice(n * blk, (n + 1) * blk)
            k = k_ref[0, rows, :]
            v = v_ref[0, rows, :]
            kaug_ref[0, rows, :] = jnp.where(head0_b, k, jnp.where(lane_b == HEAD_DIM + n, one, zero))
            kaug_ref[1, rows, :] = jnp.where(head0_b, jnp.where(lane_b == n, one, zero), k)
            vaug_ref[0, rows, :] = jnp.where(head0_b, v, one)
            vaug_ref[1, rows, :] = jnp.where(head0_b, one, v)
            kbar_ref[n:n + 1, :] = jnp.sum(k.astype(F32), axis=0, keepdims=True) / blk
        build_queries(qlo_ref, qhi_ref, 0)

    qaug_ref[...] = qnext_ref[...]
    build_queries(qlon_ref, qhin_ref, jnp.minimum(u + 1, pl.num_programs(2) - 1))

    r_iota = lax.broadcasted_iota(jnp.int32, (blk, blk), 0)
    c_iota = lax.broadcasted_iota(jnp.int32, (blk, blk), 1)
    causal = c_iota <= r_iota

    def unit(first, t):
        w = (t >= first).astype(jnp.int32)
        return w, pl.multiple_of((t - w * first) * blk, blk)

    def scores(ps, w, e, start):
        kblk = kaug_ref[e, pl.ds(start, blk), :]
        return LOG2_E * lax.dot_general(qaug_ref[ps, w, e], kblk, NT_DIMS,
                                        preferred_element_type=F32)

    def fold(s):
        return jnp.maximum(s[:, :LANES], s[:, LANES:])

    def own_start(first, w):
        return pl.multiple_of((first, npast - first)[w] * blk, blk)

    nunits = npast + 2

    def pass1_unit(ps, first, idx):
        if idx < 2:
            for e in range(2):
                s = jnp.where(causal, scores(ps, idx, e, own_start(first, idx)), NEG_BIG)
                s_ref[ps, e, npast + idx] = s
                mx_ref[ps, idx, e] = fold(s)
        else:
            w, start = unit(first, idx - 2)
            for e in range(2):
                s = scores(ps, w, e, start)
                s_ref[ps, e, idx - 2] = s
                mx_ref[ps, w, e] = jnp.maximum(mx_ref[ps, w, e], fold(s))

    def row_max(ps):
        for w in range(2):
            for e in range(2):
                m = jnp.max(mx_ref[ps, w, e], axis=1, keepdims=True)
                m_ref[ps, w, e] = jnp.broadcast_to(m, (blk, LANES))

    def pv(ps, w, e, t, start):
        m = m_ref[ps, w, e]
        prob = jnp.exp2(s_ref[ps, e, t] - jnp.concatenate([m, m], axis=1)).astype(BF16)
        return jnp.dot(prob, vaug_ref[e, pl.ds(start, blk), :], preferred_element_type=F32)

    def pass2_unit(ps, first, idx):
        if idx < 2:
            for e in range(2):
                acc_ref[ps, idx, e] = pv(ps, idx, e, npast + idx, own_start(first, idx))
        else:
            w, start = unit(first, idx - 2)
            for e in range(2):
                acc_ref[ps, w, e] += pv(ps, w, e, idx - 2, start)

    def finish(ps):
        targets = ((olo_ref, ps), (ohi_ref, MOBA_PAIRS - 1 - ps))
        for w, (o_ref, half) in enumerate(targets):
            a0 = acc_ref[ps, w, 0]
            a1 = acc_ref[ps, w, 1]
            out = jnp.where(head0, a0 / pltpu.roll(a0, HEAD_DIM, 1), a1 / pltpu.roll(a1, HEAD_DIM, 1))
            o_ref[0, half * blk:(half + 1) * blk, :] = out.astype(BF16)

    firsts = [MOBA_PAIRS * u + ps for ps in range(MOBA_PAIRS)]
    for ps in range(MOBA_PAIRS):
        for idx in range(nunits):
            pass1_unit(ps, firsts[ps], idx)
    for ps in range(MOBA_PAIRS):
        row_max(ps)
        for idx in range(nunits):
            pass2_unit(ps, firsts[ps], idx)
    for ps in range(MOBA_PAIRS):
        finish(ps)


def _moba(p):
    b, s, _ = p.shape
    nblocks = s // MOBA_BLOCK
    blk = MOBA_BLOCK
    rows = MOBA_PAIRS * blk
    steps = nblocks // 2 // MOBA_PAIRS
    assert MOBA_PAIRS == 2 and nblocks <= HEAD_DIM and nblocks % (2 * MOBA_PAIRS) == 0
    assert nblocks % SUBLANES == 0
    nxt = lambda i: jnp.minimum(i + 1, steps - 1)
    qspec = lambda row_block: pl.BlockSpec(
        (1, rows, LANES), lambda bi, hp, i: (bi, row_block(i), hp))
    kspec = pl.BlockSpec((1, s, LANES), lambda bi, hp, i: (bi, 0, HEAD_PAIRS + hp))
    vspec = pl.BlockSpec((1, s, LANES), lambda bi, hp, i: (bi, 0, 2 * HEAD_PAIRS + hp))
    half = jax.ShapeDtypeStruct((b, s // 2, WIDTH), BF16)
    last = 2 * steps - 1
    return pl.pallas_call(
        functools.partial(_moba_kernel, nblocks=nblocks),
        grid=(b, HEAD_PAIRS, steps),
        in_specs=[
            qspec(lambda i: i), qspec(lambda i: last - i),
            qspec(nxt), qspec(lambda i: last - nxt(i)),
            kspec, vspec,
        ],
        out_specs=[
            pl.BlockSpec((1, rows, LANES), lambda bi, hp, i: (bi, i, hp)),
            pl.BlockSpec((1, rows, LANES), lambda bi, hp, i: (bi, steps - 1 - i, hp)),
        ],
        out_shape=[half, half],
        scratch_shapes=[
            pltpu.VMEM((2, s, LANES), BF16),
            pltpu.VMEM((2, s, LANES), BF16),
            pltpu.VMEM((nblocks, LANES), F32),
            pltpu.VMEM((MOBA_PAIRS, 2, 2, blk, LANES), BF16),
            pltpu.VMEM((MOBA_PAIRS, 2, 2, blk, LANES), BF16),
            pltpu.VMEM((MOBA_PAIRS, 2, nblocks + 1, blk, blk), F32),
            pltpu.VMEM((MOBA_PAIRS, 2, 2, blk, LANES), F32),
            pltpu.VMEM((MOBA_PAIRS, 2, 2, blk, LANES), F32),
            pltpu.VMEM((MOBA_PAIRS, 2, 2, blk, LANES), F32),
        ],
        compiler_params=pltpu.CompilerParams(
            dimension_semantics=("arbitrary", "arbitrary", "arbitrary"),
            vmem_limit_bytes=VMEM_LIMIT),
        name="moba",
    )(p, p, p, p, p, p)


def _retention_tables():
    c = RET_CHUNK
    log_gamma = jnp.log(1.0 - 2.0 ** (-5.0 - jnp.arange(RET_HEADS, dtype=F32)))
    pos = jnp.arange(c, dtype=F32)
    diff = pos[:, None] - pos[None, :]
    inner = jnp.where(diff[None] >= 0,
                      jnp.exp(jnp.maximum(diff, 0.0)[None] * log_gamma[:, None, None]), 0.0)
    cross = jnp.exp((pos + 1.0)[None, :] * log_gamma[:, None])
    sdec = jnp.exp((c - 1.0 - pos)[None, :] * log_gamma[:, None])
    chunk = jnp.exp(c * log_gamma)

    def lanes(t):
        t = jnp.repeat(t[:, :, None], HEAD_DIM, axis=2)
        t = t.reshape(HEAD_PAIRS, 2, c, HEAD_DIM).transpose(0, 2, 1, 3)
        return t.reshape(HEAD_PAIRS, c, LANES)

    same_head = (jnp.arange(LANES)[:, None] // HEAD_DIM) == (jnp.arange(LANES)[None, :] // HEAD_DIM)
    bd = same_head.astype(F32)
    kdec = jnp.repeat(chunk.reshape(HEAD_PAIRS, 2), HEAD_DIM, axis=1)[:, :, None] * bd[None]
    return inner, lanes(cross), lanes(sdec), kdec, bd


def _retention_kernel(q_ref, k_ref, v_ref, g_ref, idec_ref, cdec_ref, sdec_ref, kdec_ref, bd_ref,
                      gain_ref, o_ref, state_ref):
    lane = lax.broadcasted_iota(jnp.int32, (RET_CHUNK, LANES), 1)
    head0 = lane < HEAD_DIM
    head0_b = lane.astype(F32).astype(BF16) < HEAD_DIM
    zero = jnp.zeros((RET_CHUNK, LANES), BF16)

    @pl.when(pl.program_id(2) == 0)
    def _():
        state_ref[...] = jnp.zeros_like(state_ref)

    def head_mean(t):
        s0 = jnp.sum(jnp.where(head0, t, 0.0), axis=1, keepdims=True)
        s1 = jnp.sum(jnp.where(head0, 0.0, t), axis=1, keepdims=True)
        return jnp.where(head0, s0, s1) / HEAD_DIM

    state = state_ref[...]
    for j in range(RET_STEP_CHUNKS):
        rows = slice(j * RET_CHUNK, (j + 1) * RET_CHUNK)
        q = q_ref[0, rows, :]
        k = k_ref[0, rows, :]
        v = v_ref[0, rows, :]
        o_cross = jnp.dot(q, state.astype(BF16), preferred_element_type=F32) * cdec_ref[0]
        parts = []
        for e in range(2):
            qe = jnp.where(head0_b, q, zero) if e == 0 else jnp.where(head0_b, zero, q)
            s = lax.dot_general(qe, k, NT_DIMS, preferred_element_type=F32) * idec_ref[e]
            parts.append(jnp.dot(s.astype(BF16), v, preferred_element_type=F32))
        o = jnp.where(head0, parts[0], parts[1]) + o_cross

        kd = (k.astype(F32) * sdec_ref[0]).astype(BF16)
        upd = lax.dot_general(kd, v, TN_DIMS, preferred_element_type=F32)
        state = state * kdec_ref[0] + upd * bd_ref[...]

        d = o - head_mean(o)
        on = d * lax.rsqrt(head_mean(d * d) + GN_EPS)
        g = g_ref[0, rows, :].astype(F32)
        y = g * (1.0 / (1.0 + jnp.exp(-g))) * on * gain_ref[...]
        o_ref[0, rows, :] = y.astype(BF16)
    state_ref[...] = state


def _retention(p, gn_gain, tables):
    b, s, _ = p.shape
    c = RET_CHUNK
    inner, cross, sdec, kdec, bd = tables

    rows = RET_STEP_CHUNKS * c

    def col(group):
        return pl.BlockSpec((1, rows, LANES), lambda bi, hp, ci: (bi, ci, group * HEAD_PAIRS + hp))

    pair = lambda shape: pl.BlockSpec(shape, lambda bi, hp, ci: (hp, 0, 0))
    return pl.pallas_call(
        _retention_kernel,
        grid=(b, HEAD_PAIRS, s // rows),
        in_specs=[
            col(3), col(4), col(5), col(6),
            pl.BlockSpec((2, c, c), lambda bi, hp, ci: (hp, 0, 0)),
            pair((1, c, LANES)), pair((1, c, LANES)), pair((1, LANES, LANES)),
            _resident((LANES, LANES)),
            pl.BlockSpec((1, LANES), lambda bi, hp, ci: (0, hp)),
        ],
        out_specs=pl.BlockSpec((1, rows, LANES), lambda bi, hp, ci: (bi, ci, hp)),
        out_shape=jax.ShapeDtypeStruct((b, s, WIDTH), BF16),
        scratch_shapes=[pltpu.VMEM((LANES, LANES), F32)],
        compiler_params=pltpu.CompilerParams(
            dimension_semantics=("arbitrary", "arbitrary", "arbitrary"),
            vmem_limit_bytes=VMEM_LIMIT),
        name="retention",
    )(p, p, p, p, inner, cross, sdec, kdec, bd, gn_gain.reshape(1, WIDTH))


def _ffn_kernel(x_ref, ya_lo_ref, ya_hi_ref, yr_ref, woa_ref, wor_ref, ln2_ref, wg_ref, wu_ref,
                cw_ref, cb_ref, wd_ref, lnf_ref, o_ref, x1_ref, h2_ref, acc_ref, carry_ref, gu_ref,
                *, final_norm):
    tm = x_ref.shape[1]

    @pl.when(pl.program_id(1) == 0)
    def _():
        carry_ref[...] = jnp.zeros_like(carry_ref)

    first_half = (pl.program_id(1) < pl.num_programs(1) // 2).astype(F32)
    pick_lo = jnp.full((tm, WIDTH), first_half, F32).astype(BF16)
    ya = ya_lo_ref[0] * pick_lo + ya_hi_ref[0] * (1 - pick_lo)
    x1 = (x_ref[0]
          + jnp.dot(ya, woa_ref[...], preferred_element_type=F32)
          + jnp.dot(yr_ref[0], wor_ref[...], preferred_element_type=F32))
    x1_ref[...] = x1
    h2_ref[...] = _rmsnorm(x1, ln2_ref[...]).astype(BF16)
    rows = lax.broadcasted_iota(jnp.int32, (tm, FFN_CHUNK), 0)
    row0 = rows == 0
    row1 = rows == 1

    def up_proj(c):
        h2 = h2_ref[...]
        cols = slice(c * FFN_CHUNK, (c + 1) * FFN_CHUNK)
        gu_ref[c % FFN_SLOTS, 0] = jnp.dot(h2, wg_ref[:, cols], preferred_element_type=F32)
        gu_ref[c % FFN_SLOTS, 1] = jnp.dot(h2, wu_ref[:, cols], preferred_element_type=F32)

    def activation(c):
        g = gu_ref[c % FFN_SLOTS, 0]
        u = gu_ref[c % FFN_SLOTS, 1]
        prev = carry_ref[c]
        carry_ref[c] = g[tm - SUBLANES:, :]
        p1 = prev[SUBLANES - 1:SUBLANES, :]
        p2 = prev[SUBLANES - 2:SUBLANES - 1, :]
        g1 = jnp.where(row0, p1, pltpu.roll(g, 1, 0))
        g2 = jnp.where(row0, p2, jnp.where(row1, p1, pltpu.roll(g, 2, 0)))
        cols = slice(c * FFN_CHUNK, (c + 1) * FFN_CHUNK)
        cw = cw_ref[:, cols]
        gc = cw[0:1, :] * g2 + cw[1:2, :] * g1 + cw[2:3, :] * g + cb_ref[:, cols]
        return (gc * (1.0 / (1.0 + jnp.exp(-gc))) * u).astype(BF16)

    for c in range(min(FFN_SLOTS, FFN_NCHUNK)):
        up_proj(c)
    for c0 in range(0, FFN_NCHUNK, 2):
        part = None
        for c in range(c0, min(c0 + 2, FFN_NCHUNK)):
            act = activation(c)
            if c + FFN_SLOTS < FFN_NCHUNK:
                up_proj(c + FFN_SLOTS)
            down = jnp.dot(act, wd_ref[c * FFN_CHUNK:(c + 1) * FFN_CHUNK, :],
                           preferred_element_type=F32)
            part = down if part is None else part + down
        acc_ref[...] = part if c0 == 0 else acc_ref[...] + part
    x2 = x1_ref[...] + acc_ref[...]
    if final_norm:
        x2 = _rmsnorm(x2, lnf_ref[...])
    o_ref[0] = x2


def _ffn(x, ya_lo, ya_hi, yr, woa, wor, ln2, wg, wu, cw, cb, wd, ln_f, final_norm):
    b, s, _ = x.shape
    tm = FFN_TILE
    nhalf = s // tm // 2
    tok = lambda width: pl.BlockSpec((1, tm, width), lambda bi, ti: (bi, ti, 0))
    lo = pl.BlockSpec((1, tm, WIDTH), lambda bi, ti: (bi, jnp.minimum(ti, nhalf - 1), 0))
    hi = pl.BlockSpec((1, tm, WIDTH), lambda bi, ti: (bi, jnp.maximum(ti - nhalf, 0), 0))
    return pl.pallas_call(
        functools.partial(_ffn_kernel, final_norm=final_norm),
        grid=(b, s // tm),
        in_specs=[
            tok(D_MODEL), lo, hi, tok(WIDTH),
            _resident((WIDTH, D_MODEL)), _resident((WIDTH, D_MODEL)),
            _resident((1, D_MODEL)),
            _resident((D_MODEL, D_FF)), _resident((D_MODEL, D_FF)),
            _resident((3, D_FF)), _resident((1, D_FF)),
            _resident((D_FF, D_MODEL)),
            _resident((1, D_MODEL)),
        ],
        out_specs=tok(D_MODEL),
        out_shape=jax.ShapeDtypeStruct((b, s, D_MODEL), F32),
        scratch_shapes=[
            pltpu.VMEM((tm, D_MODEL), F32),
            pltpu.VMEM((tm, D_MODEL), BF16),
            pltpu.VMEM((tm, D_MODEL), F32),
            pltpu.VMEM((FFN_NCHUNK, SUBLANES, FFN_CHUNK), F32),
            pltpu.VMEM((FFN_SLOTS, 2, tm, FFN_CHUNK), F32),
        ],
        compiler_params=pltpu.CompilerParams(
            dimension_semantics=("arbitrary", "arbitrary"), vmem_limit_bytes=VMEM_LIMIT),
        name="ffn",
    )(x, ya_lo, ya_hi, yr, woa, wor, ln2.reshape(1, D_MODEL), wg, wu, cw, cb.reshape(1, D_FF),
      wd,
      ln_f.reshape(1, D_MODEL))


def kernel(x, ln1, w_in, gn_gain, w_out, ln2, w_up, conv_w, conv_b, w_down, ln_f):
    b, s, d = x.shape
    depth = w_in.shape[0]
    assert d == D_MODEL
    assert s % max(IN_TILE, 2 * FFN_TILE, MOBA_BLOCK, RET_STEP_CHUNKS * RET_CHUNK) == 0

    inv_a = ROPE_THETA ** (-jnp.arange(ROPE_DIM // 2, dtype=F32) / (ROPE_DIM // 2))
    inv_r = 1.0 / (RET_ROPE_THETA ** jnp.linspace(0.0, 1.0, HEAD_DIM // 2, dtype=F32))
    tabs_a = _rope_tables(s, inv_a, ROPE_DIM // 2)
    tabs_r = _rope_tables(s, inv_r, HEAD_DIM // 2)
    ret_tables = _retention_tables()
    scale = HEAD_DIM ** -0.5
    col_scale = jnp.ones((IN_COLS,), F32).at[0:WIDTH].set(scale).at[4 * WIDTH:5 * WIDTH].set(scale)

    for l in range(depth):
        w_in_l = (w_in[l] * col_scale[None, :]).astype(BF16)
        p = _inproj(x, ln1[l], w_in_l, tabs_a, tabs_r)
        ya_lo, ya_hi = _moba(p)
        yr = _retention(p, gn_gain[l], ret_tables)
        wo = w_out[l].astype(BF16)
        x = _ffn(x, ya_lo, ya_hi, yr, wo[:WIDTH], wo[WIDTH:], ln2[l],
                 w_up[l][:, :D_FF].astype(BF16), w_up[l][:, D_FF:].astype(BF16),
                 conv_w[l], conv_b[l], w_down[l].astype(BF16),
                 ln_f, final_norm=(l == depth - 1))
    return x
```

```python
import functools

import jax
import jax.numpy as jnp
from jax import lax
from jax.experimental import pallas as pl
from jax.experimental.pallas import tpu as pltpu

D_MODEL = 1024
HEAD_DIM = 64
MOBA_HEADS = 8
RET_HEADS = 8
WIDTH = 512
IN_COLS = 7 * WIDTH
MOBA_BLOCK = 256
MOBA_TOPK = 3
ROPE_THETA = 500000.0
ROPE_DIM = HEAD_DIM // 4
RET_ROPE_THETA = 10000.0
RET_CHUNK = 256
D_FF = 2816
NORM_EPS = 1e-6
GN_EPS = 1e-5
NEG_BIG = -1e9

LANES = 128
HEAD_PAIRS = WIDTH // LANES
SUBLANES = 8
VMEM_LIMIT = 56 * 1024 * 1024

IN_TILE = 512
FFN_TILE = 512
FFN_CHUNK = 256
FFN_NCHUNK = D_FF // FFN_CHUNK
FFN_SLOTS = 6
RET_STEP_CHUNKS = 8
MOBA_PAIRS = 4
LOG2_E = 1.4426950408889634

F32 = jnp.float32
BF16 = jnp.bfloat16
NT_DIMS = (((1,), (1,)), ((), ()))
TN_DIMS = (((0,), (0,)), ((), ()))


def _rmsnorm(x, g):
    return x * lax.rsqrt(jnp.mean(x * x, axis=-1, keepdims=True) + NORM_EPS) * g


def _resident(shape):
    zeros = (0,) * len(shape)
    return pl.BlockSpec(shape, lambda *_: zeros, pipeline_mode=pl.Buffered(1))


def _rope_tables(seq, inv_freq, half):
    ang = jnp.arange(seq, dtype=F32)[:, None] * inv_freq[None, :]
    cos, sin = jnp.cos(ang), jnp.sin(ang)
    pad = HEAD_DIM - 2 * half
    ones = jnp.ones((seq, pad), F32)
    zeros = jnp.zeros((seq, pad), F32)
    zh = jnp.zeros((seq, half), F32)
    c = jnp.concatenate([cos, cos, ones], axis=1)
    su = jnp.concatenate([-sin, zh, zeros], axis=1)
    sd = jnp.concatenate([zh, sin, zeros], axis=1)
    two = lambda t: jnp.concatenate([t, t], axis=1)
    return two(c), two(su), two(sd)


def _inproj_kernel(x_ref, ln_ref, w_ref, ca_ref, sau_ref, sad_ref, cr_ref, sru_ref, srd_ref,
                   p_ref):
    h = _rmsnorm(x_ref[0], ln_ref[...]).astype(BF16)

    def rope(y, c, su, sd, half):
        outs = []
        for g in range(WIDTH // LANES):
            yg = y[:, g * LANES:(g + 1) * LANES]
            up = pltpu.roll(yg, LANES - half, 1)
            dn = pltpu.roll(yg, half, 1)
            outs.append(yg * c + up * su + dn * sd)
        return jnp.concatenate(outs, axis=1)

    for slab in range(IN_COLS // WIDTH):
        cols = slice(slab * WIDTH, (slab + 1) * WIDTH)
        y = jnp.dot(h, w_ref[:, cols], preferred_element_type=F32)
        if slab in (0, 1):
            y = rope(y, ca_ref[...], sau_ref[...], sad_ref[...], ROPE_DIM // 2)
        elif slab in (3, 4):
            y = rope(y, cr_ref[...], sru_ref[...], srd_ref[...], HEAD_DIM // 2)
        p_ref[0, :, cols] = y.astype(BF16)


def _inproj(x, ln, w_bf16, tabs_a, tabs_r):
    b, s, _ = x.shape
    tm = IN_TILE
    tab = pl.BlockSpec((tm, LANES), lambda si, bi: (si, 0))
    return pl.pallas_call(
        _inproj_kernel,
        grid=(s // tm, b),
        in_specs=[
            pl.BlockSpec((1, tm, D_MODEL), lambda si, bi: (bi, si, 0)),
            _resident((1, D_MODEL)),
            _resident((D_MODEL, IN_COLS)),
            tab, tab, tab, tab, tab, tab,
        ],
        out_specs=pl.BlockSpec((1, tm, IN_COLS), lambda si, bi: (bi, si, 0)),
        out_shape=jax.ShapeDtypeStruct((b, s, IN_COLS), BF16),
        compiler_params=pltpu.CompilerParams(
            dimension_semantics=("arbitrary", "arbitrary"), vmem_limit_bytes=VMEM_LIMIT),
        name="inproj",
    )(x, ln.reshape(1, D_MODEL), w_bf16, *tabs_a, *tabs_r)


def _moba_kernel(qlo_ref, qhi_ref, qlon_ref, qhin_ref, k_ref, v_ref, olo_ref, ohi_ref,
                 kaug_ref, vaug_ref, kbar_ref, qaug_ref, qnext_ref, s_ref, mx_ref, m_ref, acc_ref,
                 *, nblocks):
    u = pl.program_id(2)
    blk = MOBA_BLOCK
    npast = nblocks - 1
    lane = lax.broadcasted_iota(jnp.int32, (blk, LANES), 1)
    head0 = lane < HEAD_DIM
    lane_b = lane.astype(F32).astype(BF16)
    head0_b = lane_b < HEAD_DIM
    one = jnp.ones((blk, LANES), BF16)
    zero = jnp.zeros((blk, LANES), BF16)

    def build_queries(lo_ref, hi_ref, step):
        kbar = kbar_ref[...]
        kb_hi = kbar.astype(BF16)
        kb_lo = (kbar - kb_hi.astype(F32)).astype(BF16)
        kb = jnp.concatenate([kb_hi, kb_lo], axis=0)
        n_iota = lax.broadcasted_iota(jnp.int32, (nblocks, blk), 0)
        fill = jnp.zeros((HEAD_DIM - nblocks, blk), F32)
        for ps in range(MOBA_PAIRS):
            first = MOBA_PAIRS * step + ps
            sources = ((lo_ref, ps, first), (hi_ref, MOBA_PAIRS - 1 - ps, npast - first))
            for w, (q_ref, half, i) in enumerate(sources):
                q = q_ref[0, half * blk:(half + 1) * blk, :]
                past = n_iota < i
                pens = []
                for e in range(2):
                    qe = jnp.where(head0_b, q, zero) if e == 0 else jnp.where(head0_b, zero, q)
                    g2 = lax.dot_general(kb, qe, NT_DIMS, preferred_element_type=F32)
                    g = jnp.where(past, g2[:nblocks] + g2[nblocks:], NEG_BIG)
                    cnt = jnp.zeros((nblocks, blk), F32)
                    for m in range(nblocks):
                        row = g[m:m + 1, :]
                        ahead = (row > g) | ((row == g) & (n_iota > m))
                        cnt = cnt + jnp.where(ahead, 1.0, 0.0)
                    keep = ((cnt < MOBA_TOPK) & past) | (n_iota == i)
                    pens.append(jnp.where(keep, 0.0, NEG_BIG))
                pen_t = jnp.concatenate([pens[1], fill, pens[0], fill], axis=0)
                pen = pen_t.T.astype(BF16)
                qnext_ref[ps, w, 0] = jnp.where(head0_b, q, pen)
                qnext_ref[ps, w, 1] = jnp.where(head0_b, pen, q)

    @pl.when(u == 0)
    def _():
        for n in range(nblocks):
            rows = slice(n * blk, (n + 1) * blk)
            k = k_ref[0, rows, :]
            v = v_ref[0, rows, :]
            kaug_ref[0, rows, :] = jnp.where(head0_b, k, jnp.where(lane_b == HEAD_DIM + n, one, zero))
            kaug_ref[1, rows, :] = jnp.where(head0_b, jnp.where(lane_b == n, one, zero), k)
            vaug_ref[0, rows, :] = jnp.where(head0_b, v, one)
            vaug_ref[1, rows, :] = jnp.where(head0_b, one, v)
            kbar_ref[n:n + 1, :] = jnp.sum(k.astype(F32), axis=0, keepdims=True) / blk
        build_queries(qlo_ref, qhi_ref, 0)

    qaug_ref[...] = qnext_ref[...]
    build_queries(qlon_ref, qhin_ref, jnp.minimum(u + 1, pl.num_programs(2) - 1))

    r_iota = lax.broadcasted_iota(jnp.int32, (blk, blk), 0)
    c_iota = lax.broadcasted_iota(jnp.int32, (blk, blk), 1)
    causal = c_iota <= r_iota

    def unit(first, t):
        w = (t >= first).astype(jnp.int32)
        return w, pl.multiple_of((t - w * first) * blk, blk)

    def scores(ps, w, e, start):
        kblk = kaug_ref[e, pl.ds(start, blk), :]
        return LOG2_E * lax.dot_general(qaug_ref[ps, w, e], kblk, NT_DIMS,
                                        preferred_element_type=F32)

    def fold(s):
        return jnp.maximum(s[:, :LANES], s[:, LANES:])

    def own_start(first, w):
        return pl.multiple_of((first, npast - first)[w] * blk, blk)

    nunits = npast + 2

    def pass1_unit(ps, first, idx):
        if idx < 2:
            for e in range(2):
                s = jnp.where(causal, scores(ps, idx, e, own_start(first, idx)), NEG_BIG)
                s_ref[ps, e, npast + idx] = s
                mx_ref[ps, idx, e] = fold(s)
        else:
            w, start = unit(first, idx - 2)
            for e in range(2):
                s = scores(ps, w, e, start)
                s_ref[ps, e, idx - 2] = s
                mx_ref[ps, w, e] = jnp.maximum(mx_ref[ps, w, e], fold(s))

    def row_max(ps):
        for w in range(2):
            for e in range(2):
                m = jnp.max(mx_ref[ps, w, e], axis=1, keepdims=True)
                m_ref[ps, w, e] = jnp.broadcast_to(m, (blk, LANES))

    def pv(ps, w, e, t, start):
        m = m_ref[ps, w, e]
        prob = jnp.exp2(s_ref[ps, e, t] - jnp.concatenate([m, m], axis=1)).astype(BF16)
        return jnp.dot(prob, vaug_ref[e, pl.ds(start, blk), :], preferred_element_type=F32)

    def pass2_unit(ps, first, idx):
        if idx < 2:
            for e in range(2):
                acc_ref[ps, idx, e] = pv(ps, idx, e, npast + idx, own_start(first, idx))
        else:
            w, start = unit(first, idx - 2)
            for e in range(2):
                acc_ref[ps, w, e] += pv(ps, w, e, idx - 2, start)

    def finish(ps):
        targets = ((olo_ref, ps), (ohi_ref, MOBA_PAIRS - 1 - ps))
        for w, (o_ref, half) in enumerate(targets):
            a0 = acc_ref[ps, w, 0]
            a1 = acc_ref[ps, w, 1]
            out = jnp.where(head0, a0 / pltpu.roll(a0, HEAD_DIM, 1), a1 / pltpu.roll(a1, HEAD_DIM, 1))
            o_ref[0, half * blk:(half + 1) * blk, :] = out.astype(BF16)

    firsts = [MOBA_PAIRS * u + ps for ps in range(MOBA_PAIRS)]
    for ps in range(MOBA_PAIRS):
        for idx in range(nunits):
            pass1_unit(ps, firsts[ps], idx)
    for ps in range(MOBA_PAIRS):
        row_max(ps)
        for idx in range(nunits):
            pass2_unit(ps, firsts[ps], idx)
    for ps in range(MOBA_PAIRS):
        finish(ps)


def _moba(p):
    b, s, _ = p.shape
    nblocks = s // MOBA_BLOCK
    blk = MOBA_BLOCK
    rows = MOBA_PAIRS * blk
    steps = nblocks // 2 // MOBA_PAIRS
    assert nblocks <= HEAD_DIM and nblocks % (2 * MOBA_PAIRS) == 0
    assert nblocks % SUBLANES == 0
    nxt = lambda i: jnp.minimum(i + 1, steps - 1)
    qspec = lambda row_block: pl.BlockSpec(
        (1, rows, LANES), lambda bi, hp, i: (bi, row_block(i), hp))
    kspec = pl.BlockSpec((1, s, LANES), lambda bi, hp, i: (bi, 0, HEAD_PAIRS + hp))
    vspec = pl.BlockSpec((1, s, LANES), lambda bi, hp, i: (bi, 0, 2 * HEAD_PAIRS + hp))
    half = jax.ShapeDtypeStruct((b, s // 2, WIDTH), BF16)
    last = 2 * steps - 1
    return pl.pallas_call(
        functools.partial(_moba_kernel, nblocks=nblocks),
        grid=(b, HEAD_PAIRS, steps),
        in_specs=[
            qspec(lambda i: i), qspec(lambda i: last - i),
            qspec(nxt), qspec(lambda i: last - nxt(i)),
            kspec, vspec,
        ],
        out_specs=[
            pl.BlockSpec((1, rows, LANES), lambda bi, hp, i: (bi, i, hp)),
            pl.BlockSpec((1, rows, LANES), lambda bi, hp, i: (bi, steps - 1 - i, hp)),
        ],
        out_shape=[half, half],
        scratch_shapes=[
            pltpu.VMEM((2, s, LANES), BF16),
            pltpu.VMEM((2, s, LANES), BF16),
            pltpu.VMEM((nblocks, LANES), F32),
            pltpu.VMEM((MOBA_PAIRS, 2, 2, blk, LANES), BF16),
            pltpu.VMEM((MOBA_PAIRS, 2, 2, blk, LANES), BF16),
            pltpu.VMEM((MOBA_PAIRS, 2, nblocks + 1, blk, blk), F32),
            pltpu.VMEM((MOBA_PAIRS, 2, 2, blk, LANES), F32),
            pltpu.VMEM((MOBA_PAIRS, 2, 2, blk, LANES), F32),
            pltpu.VMEM((MOBA_PAIRS, 2, 2, blk, LANES), F32),
        ],
        compiler_params=pltpu.CompilerParams(
            dimension_semantics=("arbitrary", "arbitrary", "arbitrary"),
            vmem_limit_bytes=VMEM_LIMIT),
        name="moba",
    )(p, p, p, p, p, p)


def _retention_tables():
    c = RET_CHUNK
    log_gamma = jnp.log(1.0 - 2.0 ** (-5.0 - jnp.arange(RET_HEADS, dtype=F32)))
    pos = jnp.arange(c, dtype=F32)
    diff = pos[:, None] - pos[None, :]
    inner = jnp.where(diff[None] >= 0,
                      jnp.exp(jnp.maximum(diff, 0.0)[None] * log_gamma[:, None, None]), 0.0)
    cross = jnp.exp((pos + 1.0)[None, :] * log_gamma[:, None])
    sdec = jnp.exp((c - 1.0 - pos)[None, :] * log_gamma[:, None])
    chunk = jnp.exp(c * log_gamma)

    def lanes(t):
        t = jnp.repeat(t[:, :, None], HEAD_DIM, axis=2)
        t = t.reshape(HEAD_PAIRS, 2, c, HEAD_DIM).transpose(0, 2, 1, 3)
        return t.reshape(HEAD_PAIRS, c, LANES)

    same_head = (jnp.arange(LANES)[:, None] // HEAD_DIM) == (jnp.arange(LANES)[None, :] // HEAD_DIM)
    bd = same_head.astype(F32)
    kdec = jnp.repeat(chunk.reshape(HEAD_PAIRS, 2), HEAD_DIM, axis=1)[:, :, None] * bd[None]
    return inner, lanes(cross), lanes(sdec), kdec, bd


def _retention_kernel(q_ref, k_ref, v_ref, g_ref, idec_ref, cdec_ref, sdec_ref, kdec_ref, bd_ref,
                      gain_ref, o_ref, state_ref):
    lane = lax.broadcasted_iota(jnp.int32, (RET_CHUNK, LANES), 1)
    head0 = lane < HEAD_DIM
    head0_b = lane.astype(F32).astype(BF16) < HEAD_DIM
    zero = jnp.zeros((RET_CHUNK, LANES), BF16)

    @pl.when(pl.program_id(2) == 0)
    def _():
        state_ref[...] = jnp.zeros_like(state_ref)

    def head_mean(t):
        s0 = jnp.sum(jnp.where(head0, t, 0.0), axis=1, keepdims=True)
        s1 = jnp.sum(jnp.where(head0, 0.0, t), axis=1, keepdims=True)
        return jnp.where(head0, s0, s1) / HEAD_DIM

    state = state_ref[...]
    for j in range(RET_STEP_CHUNKS):
        rows = slice(j * RET_CHUNK, (j + 1) * RET_CHUNK)
        q = q_ref[0, rows, :]
        k = k_ref[0, rows, :]
        v = v_ref[0, rows, :]
        o_cross = jnp.dot(q, state.astype(BF16), preferred_element_type=F32) * cdec_ref[0]
        parts = []
        for e in range(2):
            qe = jnp.where(head0_b, q, zero) if e == 0 else jnp.where(head0_b, zero, q)
            s = lax.dot_general(qe, k, NT_DIMS, preferred_element_type=F32) * idec_ref[e]
            parts.append(jnp.dot(s.astype(BF16), v, preferred_element_type=F32))
        o = jnp.where(head0, parts[0], parts[1]) + o_cross

        kd = (k.astype(F32) * sdec_ref[0]).astype(BF16)
        upd = lax.dot_general(kd, v, TN_DIMS, preferred_element_type=F32)
        state = state * kdec_ref[0] + upd * bd_ref[...]

        d = o - head_mean(o)
        on = d * lax.rsqrt(head_mean(d * d) + GN_EPS)
        g = g_ref[0, rows, :].astype(F32)
        y = g * (1.0 / (1.0 + jnp.exp(-g))) * on * gain_ref[...]
        o_ref[0, rows, :] = y.astype(BF16)
    state_ref[...] = state


def _retention(p, gn_gain, tables):
    b, s, _ = p.shape
    c = RET_CHUNK
    inner, cross, sdec, kdec, bd = tables

    rows = RET_STEP_CHUNKS * c

    def col(group):
        return pl.BlockSpec((1, rows, LANES), lambda bi, hp, ci: (bi, ci, group * HEAD_PAIRS + hp))

    pair = lambda shape: pl.BlockSpec(shape, lambda bi, hp, ci: (hp, 0, 0))
    return pl.pallas_call(
        _retention_kernel,
        grid=(b, HEAD_PAIRS, s // rows),
        in_specs=[
            col(3), col(4), col(5), col(6),
            pl.BlockSpec((2, c, c), lambda bi, hp, ci: (hp, 0, 0)),
            pair((1, c, LANES)), pair((1, c, LANES)), pair((1, LANES, LANES)),
            _resident((LANES, LANES)),
            pl.BlockSpec((1, LANES), lambda bi, hp, ci: (0, hp)),
        ],
        out_specs=pl.BlockSpec((1, rows, LANES), lambda bi, hp, ci: (bi, ci, hp)),
        out_shape=jax.ShapeDtypeStruct((b, s, WIDTH), BF16),
        scratch_shapes=[pltpu.VMEM((LANES, LANES), F32)],
        compiler_params=pltpu.CompilerParams(
            dimension_semantics=("arbitrary", "arbitrary", "arbitrary"),
            vmem_limit_bytes=VMEM_LIMIT),
        name="retention",
    )(p, p, p, p, inner, cross, sdec, kdec, bd, gn_gain.reshape(1, WIDTH))


def _ffn_kernel(x_ref, ya_lo_ref, ya_hi_ref, yr_ref, woa_ref, wor_ref, ln2_ref, wg_ref, wu_ref,
                cw_ref, cb_ref, wd_ref, lnf_ref, o_ref, x1_ref, h2_ref, acc_ref, carry_ref, gu_ref,
                *, final_norm):
    tm = x_ref.shape[1]

    @pl.when(pl.program_id(1) == 0)
    def _():
        carry_ref[...] = jnp.zeros_like(carry_ref)

    first_half = (pl.program_id(1) < pl.num_programs(1) // 2).astype(F32)
    pick_lo = jnp.full((tm, WIDTH), first_half, F32).astype(BF16)
    ya = ya_lo_ref[0] * pick_lo + ya_hi_ref[0] * (1 - pick_lo)
    x1 = (x_ref[0]
          + jnp.dot(ya, woa_ref[...], preferred_element_type=F32)
          + jnp.dot(yr_ref[0], wor_ref[...], preferred_element_type=F32))
    x1_ref[...] = x1
    h2_ref[...] = _rmsnorm(x1, ln2_ref[...]).astype(BF16)
    rows = lax.broadcasted_iota(jnp.int32, (tm, FFN_CHUNK), 0)
    row0 = rows == 0
    row1 = rows == 1

    def up_proj(c):
        h2 = h2_ref[...]
        cols = slice(c * FFN_CHUNK, (c + 1) * FFN_CHUNK)
        gu_ref[c % FFN_SLOTS, 0] = jnp.dot(h2, wg_ref[:, cols], preferred_element_type=F32)
        gu_ref[c % FFN_SLOTS, 1] = jnp.dot(h2, wu_ref[:, cols], preferred_element_type=F32)

    def activation(c):
        g = gu_ref[c % FFN_SLOTS, 0]
        u = gu_ref[c % FFN_SLOTS, 1]
        prev = carry_ref[c]
        carry_ref[c] = g[tm - SUBLANES:, :]
        p1 = prev[SUBLANES - 1:SUBLANES, :]
        p2 = prev[SUBLANES - 2:SUBLANES - 1, :]
        g1 = jnp.where(row0, p1, pltpu.roll(g, 1, 0))
        g2 = jnp.where(row0, p2, jnp.where(row1, p1, pltpu.roll(g, 2, 0)))
        cols = slice(c * FFN_CHUNK, (c + 1) * FFN_CHUNK)
        cw = cw_ref[:, cols]
        gc = cw[0:1, :] * g2 + cw[1:2, :] * g1 + cw[2:3, :] * g + cb_ref[:, cols]
        return (gc * (1.0 / (1.0 + jnp.exp(-gc))) * u).astype(BF16)

    for c in range(min(FFN_SLOTS, FFN_NCHUNK)):
        up_proj(c)
    for c0 in range(0, FFN_NCHUNK, 2):
        part = None
        for c in range(c0, min(c0 + 2, FFN_NCHUNK)):
            act = activation(c)
            if c + FFN_SLOTS < FFN_NCHUNK:
                up_proj(c + FFN_SLOTS)
            down = jnp.dot(act, wd_ref[c * FFN_CHUNK:(c + 1) * FFN_CHUNK, :],
                           preferred_element_type=F32)
            part = down if part is None else part + down
        acc_ref[...] = part if c0 == 0 else acc_ref[...] + part
    x2 = x1_ref[...] + acc_ref[...]
    if final_norm:
        x2 = _rmsnorm(x2, lnf_ref[...])
    o_ref[0] = x2


def _ffn(x, ya_lo, ya_hi, yr, woa, wor, ln2, wg, wu, cw, cb, wd, ln_f, final_norm):
    b, s, _ = x.shape
    tm = FFN_TILE
    nhalf = s // tm // 2
    tok = lambda width: pl.BlockSpec((1, tm, width), lambda bi, ti: (bi, ti, 0))
    lo = pl.BlockSpec((1, tm, WIDTH), lambda bi, ti: (bi, jnp.minimum(ti, nhalf - 1), 0))
    hi = pl.BlockSpec((1, tm, WIDTH), lambda bi, ti: (bi, jnp.maximum(ti - nhalf, 0), 0))
    return pl.pallas_call(
        functools.partial(_ffn_kernel, final_norm=final_norm),
        grid=(b, s // tm),
        in_specs=[
            tok(D_MODEL), lo, hi, tok(WIDTH),
            _resident((WIDTH, D_MODEL)), _resident((WIDTH, D_MODEL)),
            _resident((1, D_MODEL)),
            _resident((D_MODEL, D_FF)), _resident((D_MODEL, D_FF)),
            _resident((3, D_FF)), _resident((1, D_FF)),
            _resident((D_FF, D_MODEL)),
            _resident((1, D_MODEL)),
        ],
        out_specs=tok(D_MODEL),
        out_shape=jax.ShapeDtypeStruct((b, s, D_MODEL), F32),
        scratch_shapes=[
            pltpu.VMEM((tm, D_MODEL), F32),
            pltpu.VMEM((tm, D_MODEL), BF16),
            pltpu.VMEM((tm, D_MODEL), F32),
            pltpu.VMEM((FFN_NCHUNK, SUBLANES, FFN_CHUNK), F32),
            pltpu.VMEM((FFN_SLOTS, 2, tm, FFN_CHUNK), F32),
        ],
        compiler_params=pltpu.CompilerParams(
            dimension_semantics=("arbitrary", "arbitrary"), vmem_limit_bytes=VMEM_LIMIT),
        name="ffn",
    )(x, ya_lo, ya_hi, yr, woa, wor, ln2.reshape(1, D_MODEL), wg, wu, cw, cb.reshape(1, D_FF),
      wd,
      ln_f.reshape(1, D_MODEL))


def kernel(x, ln1, w_in, gn_gain, w_out, ln2, w_up, conv_w, conv_b, w_down, ln_f):
    b, s, d = x.shape
    depth = w_in.shape[0]
    assert d == D_MODEL
    assert s % max(IN_TILE, 2 * FFN_TILE, MOBA_BLOCK, RET_STEP_CHUNKS * RET_CHUNK) == 0

    inv_a = ROPE_THETA ** (-jnp.arange(ROPE_DIM // 2, dtype=F32) / (ROPE_DIM // 2))
    inv_r = 1.0 / (RET_ROPE_THETA ** jnp.linspace(0.0, 1.0, HEAD_DIM // 2, dtype=F32))
    tabs_a = _rope_tables(s, inv_a, ROPE_DIM // 2)
    tabs_r = _rope_tables(s, inv_r, HEAD_DIM // 2)
    ret_tables = _retention_tables()
    scale = HEAD_DIM ** -0.5
    col_scale = jnp.ones((IN_COLS,), F32).at[0:WIDTH].set(scale).at[4 * WIDTH:5 * WIDTH].set(scale)

    for l in range(depth):
        w_in_l = (w_in[l] * col_scale[None, :]).astype(BF16)
        p = _inproj(x, ln1[l], w_in_l, tabs_a, tabs_r)
        ya_lo, ya_hi = _moba(p)
        yr = _retention(p, gn_gain[l], ret_tables)
        wo = w_out[l].astype(BF16)
        x = _ffn(x, ya_lo, ya_hi, yr, wo[:WIDTH], wo[WIDTH:], ln2[l],
                 w_up[l][:, :D_FF].astype(BF16), w_up[l][:, D_FF:].astype(BF16),
                 conv_w[l], conv_b[l], w_down[l].astype(BF16),
                 ln_f, final_norm=(l == depth - 1))
    return x
```

```python
import functools

import jax
import jax.numpy as jnp
from jax import lax
from jax.experimental import pallas as pl
from jax.experimental.pallas import tpu as pltpu

D_MODEL = 1024
HEAD_DIM = 64
MOBA_HEADS = 8
RET_HEADS = 8
WIDTH = 512
IN_COLS = 7 * WIDTH
MOBA_BLOCK = 256
MOBA_TOPK = 3
ROPE_THETA = 500000.0
ROPE_DIM = HEAD_DIM // 4
RET_ROPE_THETA = 10000.0
RET_CHUNK = 256
D_FF = 2816
NORM_EPS = 1e-6
GN_EPS = 1e-5
NEG_BIG = -1e9

LANES = 128
HEAD_PAIRS = WIDTH // LANES
SUBLANES = 8
VMEM_LIMIT = 56 * 1024 * 1024

IN_TILE = 512
FFN_TILE = 512
FFN_CHUNK = 256
FFN_NCHUNK = D_FF // FFN_CHUNK
FFN_SLOTS = 6
RET_STEP_CHUNKS = 8
MOBA_PAIRS = 4
LOG2_E = 1.4426950408889634

F32 = jnp.float32
BF16 = jnp.bfloat16
NT_DIMS = (((1,), (1,)), ((), ()))
TN_DIMS = (((0,), (0,)), ((), ()))


def _rmsnorm(x, g):
    return x * lax.rsqrt(jnp.mean(x * x, axis=-1, keepdims=True) + NORM_EPS) * g


def _resident(shape):
    zeros = (0,) * len(shape)
    return pl.BlockSpec(shape, lambda *_: zeros, pipeline_mode=pl.Buffered(1))


def _rope_tables(seq, inv_freq, half):
    ang = jnp.arange(seq, dtype=F32)[:, None] * inv_freq[None, :]
    cos, sin = jnp.cos(ang), jnp.sin(ang)
    pad = HEAD_DIM - 2 * half
    ones = jnp.ones((seq, pad), F32)
    zeros = jnp.zeros((seq, pad), F32)
    zh = jnp.zeros((seq, half), F32)
    c = jnp.concatenate([cos, cos, ones], axis=1)
    su = jnp.concatenate([-sin, zh, zeros], axis=1)
    sd = jnp.concatenate([zh, sin, zeros], axis=1)
    two = lambda t: jnp.concatenate([t, t], axis=1)
    return two(c), two(su), two(sd)


def _inproj_kernel(x_ref, ln_ref, w_ref, ca_ref, sau_ref, sad_ref, cr_ref, sru_ref, srd_ref,
                   p_ref):
    h = _rmsnorm(x_ref[0], ln_ref[...]).astype(BF16)

    def rope(y, c, su, sd, half):
        outs = []
        for g in range(WIDTH // LANES):
            yg = y[:, g * LANES:(g + 1) * LANES]
            up = pltpu.roll(yg, LANES - half, 1)
            dn = pltpu.roll(yg, half, 1)
            outs.append(yg * c + up * su + dn * sd)
        return jnp.concatenate(outs, axis=1)

    for slab in range(IN_COLS // WIDTH):
        cols = slice(slab * WIDTH, (slab + 1) * WIDTH)
        y = jnp.dot(h, w_ref[:, cols], preferred_element_type=F32)
        if slab in (0, 1):
            y = rope(y, ca_ref[...], sau_ref[...], sad_ref[...], ROPE_DIM // 2)
        elif slab in (3, 4):
            y = rope(y, cr_ref[...], sru_ref[...], srd_ref[...], HEAD_DIM // 2)
        p_ref[0, :, cols] = y.astype(BF16)


def _inproj(x, ln, w_bf16, tabs_a, tabs_r):
    b, s, _ = x.shape
    tm = IN_TILE
    tab = pl.BlockSpec((tm, LANES), lambda si, bi: (si, 0))
    return pl.pallas_call(
        _inproj_kernel,
        grid=(s // tm, b),
        in_specs=[
            pl.BlockSpec((1, tm, D_MODEL), lambda si, bi: (bi, si, 0)),
            _resident((1, D_MODEL)),
            _resident((D_MODEL, IN_COLS)),
            tab, tab, tab, tab, tab, tab,
        ],
        out_specs=pl.BlockSpec((1, tm, IN_COLS), lambda si, bi: (bi, si, 0)),
        out_shape=jax.ShapeDtypeStruct((b, s, IN_COLS), BF16),
        compiler_params=pltpu.CompilerParams(
            dimension_semantics=("arbitrary", "arbitrary"), vmem_limit_bytes=VMEM_LIMIT),
        name="inproj",
    )(x, ln.reshape(1, D_MODEL), w_bf16, *tabs_a, *tabs_r)


def _moba_kernel(qlo_ref, qhi_ref, qlon_ref, qhin_ref, k_ref, v_ref, olo_ref, ohi_ref,
                 kaug_ref, vaug_ref, kbar_ref, qaug_ref, qnext_ref, s_ref, mx_ref, m_ref, acc_ref,
                 *, nblocks):
    u = pl.program_id(2)
    blk = MOBA_BLOCK
    npast = nblocks - 1
    lane = lax.broadcasted_iota(jnp.int32, (blk, LANES), 1)
    head0 = lane < HEAD_DIM
    lane_b = lane.astype(F32).astype(BF16)
    head0_b = lane_b < HEAD_DIM
    one = jnp.ones((blk, LANES), BF16)
    zero = jnp.zeros((blk, LANES), BF16)

    def build_queries(lo_ref, hi_ref, step):
        kbar = kbar_ref[...]
        kb_hi = kbar.astype(BF16)
        kb_lo = (kbar - kb_hi.astype(F32)).astype(BF16)
        kb = jnp.concatenate([kb_hi, kb_lo], axis=0)
        n_iota = lax.broadcasted_iota(jnp.int32, (nblocks, blk), 0)
        fill = jnp.zeros((HEAD_DIM - nblocks, blk), F32)
        for ps in range(MOBA_PAIRS):
            first = MOBA_PAIRS * step + ps
            sources = ((lo_ref, ps, first), (hi_ref, MOBA_PAIRS - 1 - ps, npast - first))
            for w, (q_ref, half, i) in enumerate(sources):
                q = q_ref[0, half * blk:(half + 1) * blk, :]
                past = n_iota < i
                pens = []
                for e in range(2):
                    qe = jnp.where(head0_b, q, zero) if e == 0 else jnp.where(head0_b, zero, q)
                    g2 = lax.dot_general(kb, qe, NT_DIMS, preferred_element_type=F32)
                    g = jnp.where(past, g2[:nblocks] + g2[nblocks:], NEG_BIG)
                    cnt = jnp.zeros((nblocks, blk), F32)
                    for m in range(nblocks):
                        row = g[m:m + 1, :]
                        ahead = (row > g) | ((row == g) & (n_iota > m))
                        cnt = cnt + jnp.where(ahead, 1.0, 0.0)
                    keep = ((cnt < MOBA_TOPK) & past) | (n_iota == i)
                    pens.append(jnp.where(keep, 0.0, NEG_BIG))
                pen_t = jnp.concatenate([pens[1], fill, pens[0], fill], axis=0)
                pen = pen_t.T.astype(BF16)
                qnext_ref[ps, w, 0] = jnp.where(head0_b, q, pen)
                qnext_ref[ps, w, 1] = jnp.where(head0_b, pen, q)

    @pl.when(u == 0)
    def _():
        for n in range(nblocks):
            rows = slice(n * blk, (n + 1) * blk)
            k = k_ref[0, rows, :]
            v = v_ref[0, rows, :]
            ka0 = jnp.where(head0_b, k, jnp.where(lane_b == HEAD_DIM + n, one, zero))
            ka1 = jnp.where(head0_b, jnp.where(lane_b == n, one, zero), k)
            kaug_ref[0, n] = ka0.astype(F32).T.astype(BF16)
            kaug_ref[1, n] = ka1.astype(F32).T.astype(BF16)
            vaug_ref[0, rows, :] = jnp.where(head0_b, v, one)
            vaug_ref[1, rows, :] = jnp.where(head0_b, one, v)
            kbar_ref[n:n + 1, :] = jnp.sum(k.astype(F32), axis=0, keepdims=True) / blk
        build_queries(qlo_ref, qhi_ref, 0)

    qaug_ref[...] = qnext_ref[...]
    build_queries(qlon_ref, qhin_ref, jnp.minimum(u + 1, pl.num_programs(2) - 1))

    r_iota = lax.broadcasted_iota(jnp.int32, (blk, blk), 0)
    c_iota = lax.broadcasted_iota(jnp.int32, (blk, blk), 1)
    causal = c_iota <= r_iota

    def unit(first, t):
        w = (t >= first).astype(jnp.int32)
        return w, pl.multiple_of((t - w * first) * blk, blk)

    def scores(ps, w, e, start):
        kblk_t = kaug_ref[e, start // blk]
        return LOG2_E * jnp.dot(qaug_ref[ps, w, e], kblk_t, preferred_element_type=F32)

    def fold(s):
        return jnp.maximum(s[:, :LANES], s[:, LANES:])

    def own_start(first, w):
        return pl.multiple_of((first, npast - first)[w] * blk, blk)

    nunits = npast + 2

    def pass1_unit(ps, first, idx):
        if idx < 2:
            for e in range(2):
                s = jnp.where(causal, scores(ps, idx, e, own_start(first, idx)), NEG_BIG)
                s_ref[ps, e, npast + idx] = s
                mx_ref[ps, idx, e] = fold(s)
        else:
            w, start = unit(first, idx - 2)
            for e in range(2):
                s = scores(ps, w, e, start)
                s_ref[ps, e, idx - 2] = s
                mx_ref[ps, w, e] = jnp.maximum(mx_ref[ps, w, e], fold(s))

    def row_max(ps):
        for w in range(2):
            for e in range(2):
                m = jnp.max(mx_ref[ps, w, e], axis=1, keepdims=True)
                m_ref[ps, w, e] = jnp.broadcast_to(m, (blk, LANES))

    def pv(ps, w, e, t, start):
        m = m_ref[ps, w, e]
        prob = jnp.exp2(s_ref[ps, e, t] - jnp.concatenate([m, m], axis=1)).astype(BF16)
        return jnp.dot(prob, vaug_ref[e, pl.ds(start, blk), :], preferred_element_type=F32)

    def pass2_unit(ps, first, idx):
        if idx < 2:
            for e in range(2):
                acc_ref[ps, idx, e] = pv(ps, idx, e, npast + idx, own_start(first, idx))
        else:
            w, start = unit(first, idx - 2)
            for e in range(2):
                acc_ref[ps, w, e] += pv(ps, w, e, idx - 2, start)

    def finish(ps):
        targets = ((olo_ref, ps), (ohi_ref, MOBA_PAIRS - 1 - ps))
        for w, (o_ref, half) in enumerate(targets):
            a0 = acc_ref[ps, w, 0]
            a1 = acc_ref[ps, w, 1]
            out = jnp.where(head0, a0 / pltpu.roll(a0, HEAD_DIM, 1), a1 / pltpu.roll(a1, HEAD_DIM, 1))
            o_ref[0, half * blk:(half + 1) * blk, :] = out.astype(BF16)

    firsts = [MOBA_PAIRS * u + ps for ps in range(MOBA_PAIRS)]
    for ps in range(MOBA_PAIRS):
        for idx in range(nunits):
            pass1_unit(ps, firsts[ps], idx)
    for ps in range(MOBA_PAIRS):
        row_max(ps)
        for idx in range(nunits):
            pass2_unit(ps, firsts[ps], idx)
    for ps in range(MOBA_PAIRS):
        finish(ps)


def _moba(p):
    b, s, _ = p.shape
    nblocks = s // MOBA_BLOCK
    blk = MOBA_BLOCK
    rows = MOBA_PAIRS * blk
    steps = nblocks // 2 // MOBA_PAIRS
    assert nblocks <= HEAD_DIM and nblocks % (2 * MOBA_PAIRS) == 0
    assert nblocks % SUBLANES == 0
    nxt = lambda i: jnp.minimum(i + 1, steps - 1)
    qspec = lambda row_block: pl.BlockSpec(
        (1, rows, LANES), lambda bi, hp, i: (bi, row_block(i), hp))
    kspec = pl.BlockSpec((1, s, LANES), lambda bi, hp, i: (bi, 0, HEAD_PAIRS + hp))
    vspec = pl.BlockSpec((1, s, LANES), lambda bi, hp, i: (bi, 0, 2 * HEAD_PAIRS + hp))
    half = jax.ShapeDtypeStruct((b, s // 2, WIDTH), BF16)
    last = 2 * steps - 1
    return pl.pallas_call(
        functools.partial(_moba_kernel, nblocks=nblocks),
        grid=(b, HEAD_PAIRS, steps),
        in_specs=[
            qspec(lambda i: i), qspec(lambda i: last - i),
            qspec(nxt), qspec(lambda i: last - nxt(i)),
            kspec, vspec,
        ],
        out_specs=[
            pl.BlockSpec((1, rows, LANES), lambda bi, hp, i: (bi, i, hp)),
            pl.BlockSpec((1, rows, LANES), lambda bi, hp, i: (bi, steps - 1 - i, hp)),
        ],
        out_shape=[half, half],
        scratch_shapes=[
            pltpu.VMEM((2, nblocks, LANES, blk), BF16),
            pltpu.VMEM((2, s, LANES), BF16),
            pltpu.VMEM((nblocks, LANES), F32),
            pltpu.VMEM((MOBA_PAIRS, 2, 2, blk, LANES), BF16),
            pltpu.VMEM((MOBA_PAIRS, 2, 2, blk, LANES), BF16),
            pltpu.VMEM((MOBA_PAIRS, 2, nblocks + 1, blk, blk), F32),
            pltpu.VMEM((MOBA_PAIRS, 2, 2, blk, LANES), F32),
            pltpu.VMEM((MOBA_PAIRS, 2, 2, blk, LANES), F32),
            pltpu.VMEM((MOBA_PAIRS, 2, 2, blk, LANES), F32),
        ],
        compiler_params=pltpu.CompilerParams(
            dimension_semantics=("arbitrary", "arbitrary", "arbitrary"),
            vmem_limit_bytes=VMEM_LIMIT),
        name="moba",
    )(p, p, p, p, p, p)


def _retention_tables():
    c = RET_CHUNK
    log_gamma = jnp.log(1.0 - 2.0 ** (-5.0 - jnp.arange(RET_HEADS, dtype=F32)))
    pos = jnp.arange(c, dtype=F32)
    diff = pos[:, None] - pos[None, :]
    inner = jnp.where(diff[None] >= 0,
                      jnp.exp(jnp.maximum(diff, 0.0)[None] * log_gamma[:, None, None]), 0.0)
    cross = jnp.exp((pos + 1.0)[None, :] * log_gamma[:, None])
    sdec = jnp.exp((c - 1.0 - pos)[None, :] * log_gamma[:, None])
    chunk = jnp.exp(c * log_gamma)

    def lanes(t):
        t = jnp.repeat(t[:, :, None], HEAD_DIM, axis=2)
        t = t.reshape(HEAD_PAIRS, 2, c, HEAD_DIM).transpose(0, 2, 1, 3)
        return t.reshape(HEAD_PAIRS, c, LANES)

    same_head = (jnp.arange(LANES)[:, None] // HEAD_DIM) == (jnp.arange(LANES)[None, :] // HEAD_DIM)
    bd = same_head.astype(F32)
    kdec = jnp.repeat(chunk.reshape(HEAD_PAIRS, 2), HEAD_DIM, axis=1)[:, :, None] * bd[None]
    return inner, lanes(cross), lanes(sdec), kdec, bd


def _retention_kernel(q_ref, k_ref, v_ref, g_ref, idec_ref, cdec_ref, sdec_ref, kdec_ref, bd_ref,
                      gain_ref, o_ref, state_ref):
    lane = lax.broadcasted_iota(jnp.int32, (RET_CHUNK, LANES), 1)
    head0 = lane < HEAD_DIM
    head0_b = lane.astype(F32).astype(BF16) < HEAD_DIM
    zero = jnp.zeros((RET_CHUNK, LANES), BF16)

    @pl.when(pl.program_id(2) == 0)
    def _():
        state_ref[...] = jnp.zeros_like(state_ref)

    def head_mean(t):
        s0 = jnp.sum(jnp.where(head0, t, 0.0), axis=1, keepdims=True)
        s1 = jnp.sum(jnp.where(head0, 0.0, t), axis=1, keepdims=True)
        return jnp.where(head0, s0, s1) / HEAD_DIM

    state = state_ref[...]
    for j in range(RET_STEP_CHUNKS):
        rows = slice(j * RET_CHUNK, (j + 1) * RET_CHUNK)
        q = q_ref[0, rows, :]
        k = k_ref[0, rows, :]
        v = v_ref[0, rows, :]
        o_cross = jnp.dot(q, state.astype(BF16), preferred_element_type=F32) * cdec_ref[0]
        parts = []
        for e in range(2):
            qe = jnp.where(head0_b, q, zero) if e == 0 else jnp.where(head0_b, zero, q)
            s = lax.dot_general(qe, k, NT_DIMS, preferred_element_type=F32) * idec_ref[e]
            parts.append(jnp.dot(s.astype(BF16), v, preferred_element_type=F32))
        o = jnp.where(head0, parts[0], parts[1]) + o_cross

        kd = (k.astype(F32) * sdec_ref[0]).astype(BF16)
        upd = lax.dot_general(kd, v, TN_DIMS, preferred_element_type=F32)
        state = state * kdec_ref[0] + upd * bd_ref[...]

        d = o - head_mean(o)
        on = d * lax.rsqrt(head_mean(d * d) + GN_EPS)
        g = g_ref[0, rows, :].astype(F32)
        y = g * (1.0 / (1.0 + jnp.exp(-g))) * on * gain_ref[...]
        o_ref[0, rows, :] = y.astype(BF16)
    state_ref[...] = state


def _retention(p, gn_gain, tables):
    b, s, _ = p.shape
    c = RET_CHUNK
    inner, cross, sdec, kdec, bd = tables

    rows = RET_STEP_CHUNKS * c

    def col(group):
        return pl.BlockSpec((1, rows, LANES), lambda bi, hp, ci: (bi, ci, group * HEAD_PAIRS + hp))

    pair = lambda shape: pl.BlockSpec(shape, lambda bi, hp, ci: (hp, 0, 0))
    return pl.pallas_call(
        _retention_kernel,
        grid=(b, HEAD_PAIRS, s // rows),
        in_specs=[
            col(3), col(4), col(5), col(6),
            pl.BlockSpec((2, c, c), lambda bi, hp, ci: (hp, 0, 0)),
            pair((1, c, LANES)), pair((1, c, LANES)), pair((1, LANES, LANES)),
            _resident((LANES, LANES)),
            pl.BlockSpec((1, LANES), lambda bi, hp, ci: (0, hp)),
        ],
        out_specs=pl.BlockSpec((1, rows, LANES), lambda bi, hp, ci: (bi, ci, hp)),
        out_shape=jax.ShapeDtypeStruct((b, s, WIDTH), BF16),
        scratch_shapes=[pltpu.VMEM((LANES, LANES), F32)],
        compiler_params=pltpu.CompilerParams(
            dimension_semantics=("arbitrary", "arbitrary", "arbitrary"),
            vmem_limit_bytes=VMEM_LIMIT),
        name="retention",
    )(p, p, p, p, inner, cross, sdec, kdec, bd, gn_gain.reshape(1, WIDTH))


def _ffn_kernel(x_ref, ya_lo_ref, ya_hi_ref, yr_ref, woa_ref, wor_ref, ln2_ref, wg_ref, wu_ref,
                cw_ref, cb_ref, wd_ref, lnf_ref, o_ref, x1_ref, h2_ref, acc_ref, carry_ref, gu_ref,
                *, final_norm):
    tm = x_ref.shape[1]

    @pl.when(pl.program_id(1) == 0)
    def _():
        carry_ref[...] = jnp.zeros_like(carry_ref)

    first_half = (pl.program_id(1) < pl.num_programs(1) // 2).astype(F32)
    pick_lo = jnp.full((tm, WIDTH), first_half, F32).astype(BF16)
    ya = ya_lo_ref[0] * pick_lo + ya_hi_ref[0] * (1 - pick_lo)
    x1 = (x_ref[0]
          + jnp.dot(ya, woa_ref[...], preferred_element_type=F32)
          + jnp.dot(yr_ref[0], wor_ref[...], preferred_element_type=F32))
    x1_ref[...] = x1
    h2_ref[...] = _rmsnorm(x1, ln2_ref[...]).astype(BF16)
    rows = lax.broadcasted_iota(jnp.int32, (tm, FFN_CHUNK), 0)
    row0 = rows == 0
    row1 = rows == 1

    def up_proj(c):
        h2 = h2_ref[...]
        cols = slice(c * FFN_CHUNK, (c + 1) * FFN_CHUNK)
        gu_ref[c % FFN_SLOTS, 0] = jnp.dot(h2, wg_ref[:, cols], preferred_element_type=F32)
        gu_ref[c % FFN_SLOTS, 1] = jnp.dot(h2, wu_ref[:, cols], preferred_element_type=F32)

    def activation(c):
        g = gu_ref[c % FFN_SLOTS, 0]
        u = gu_ref[c % FFN_SLOTS, 1]
        prev = carry_ref[c]
        carry_ref[c] = g[tm - SUBLANES:, :]
        p1 = prev[SUBLANES - 1:SUBLANES, :]
        p2 = prev[SUBLANES - 2:SUBLANES - 1, :]
        g1 = jnp.where(row0, p1, pltpu.roll(g, 1, 0))
        g2 = jnp.where(row0, p2, jnp.where(row1, p1, pltpu.roll(g, 2, 0)))
        cols = slice(c * FFN_CHUNK, (c + 1) * FFN_CHUNK)
        cw = cw_ref[:, cols]
        gc = cw[0:1, :] * g2 + cw[1:2, :] * g1 + cw[2:3, :] * g + cb_ref[:, cols]
        return (gc * (1.0 / (1.0 + jnp.exp(-gc))) * u).astype(BF16)

    for c in range(min(FFN_SLOTS, FFN_NCHUNK)):
        up_proj(c)
    for c0 in range(0, FFN_NCHUNK, 2):
        part = None
        for c in range(c0, min(c0 + 2, FFN_NCHUNK)):
            act = activation(c)
            if c + FFN_SLOTS < FFN_NCHUNK:
                up_proj(c + FFN_SLOTS)
            down = jnp.dot(act, wd_ref[c * FFN_CHUNK:(c + 1) * FFN_CHUNK, :],
                           preferred_element_type=F32)
            part = down if part is None else part + down
        acc_ref[...] = part if c0 == 0 else acc_ref[...] + part
    x2 = x1_ref[...] + acc_ref[...]
    if final_norm:
        x2 = _rmsnorm(x2, lnf_ref[...])
    o_ref[0] = x2


def _ffn(x, ya_lo, ya_hi, yr, woa, wor, ln2, wg, wu, cw, cb, wd, ln_f, final_norm):
    b, s, _ = x.shape
    tm = FFN_TILE
    nhalf = s // tm // 2
    tok = lambda width: pl.BlockSpec((1, tm, width), lambda bi, ti: (bi, ti, 0))
    lo = pl.BlockSpec((1, tm, WIDTH), lambda bi, ti: (bi, jnp.minimum(ti, nhalf - 1), 0))
    hi = pl.BlockSpec((1, tm, WIDTH), lambda bi, ti: (bi, jnp.maximum(ti - nhalf, 0), 0))
    return pl.pallas_call(
        functools.partial(_ffn_kernel, final_norm=final_norm),
        grid=(b, s // tm),
        in_specs=[
            tok(D_MODEL), lo, hi, tok(WIDTH),
            _resident((WIDTH, D_MODEL)), _resident((WIDTH, D_MODEL)),
            _resident((1, D_MODEL)),
            _resident((D_MODEL, D_FF)), _resident((D_MODEL, D_FF)),
            _resident((3, D_FF)), _resident((1, D_FF)),
            _resident((D_FF, D_MODEL)),
            _resident((1, D_MODEL)),
        ],
        out_specs=tok(D_MODEL),
        out_shape=jax.ShapeDtypeStruct((b, s, D_MODEL), F32),
        scratch_shapes=[
            pltpu.VMEM((tm, D_MODEL), F32),
            pltpu.VMEM((tm, D_MODEL), BF16),
            pltpu.VMEM((tm, D_MODEL), F32),
            pltpu.VMEM((FFN_NCHUNK, SUBLANES, FFN_CHUNK), F32),
            pltpu.VMEM((FFN_SLOTS, 2, tm, FFN_CHUNK), F32),
        ],
        compiler_params=pltpu.CompilerParams(
            dimension_semantics=("arbitrary", "arbitrary"), vmem_limit_bytes=VMEM_LIMIT),
        name="ffn",
    )(x, ya_lo, ya_hi, yr, woa, wor, ln2.reshape(1, D_MODEL), wg, wu, cw, cb.reshape(1, D_FF),
      wd,
      ln_f.reshape(1, D_MODEL))


def kernel(x, ln1, w_in, gn_gain, w_out, ln2, w_up, conv_w, conv_b, w_down, ln_f):
    b, s, d = x.shape
    depth = w_in.shape[0]
    assert d == D_MODEL
    assert s % max(IN_TILE, 2 * FFN_TILE, MOBA_BLOCK, RET_STEP_CHUNKS * RET_CHUNK) == 0

    inv_a = ROPE_THETA ** (-jnp.arange(ROPE_DIM // 2, dtype=F32) / (ROPE_DIM // 2))
    inv_r = 1.0 / (RET_ROPE_THETA ** jnp.linspace(0.0, 1.0, HEAD_DIM // 2, dtype=F32))
    tabs_a = _rope_tables(s, inv_a, ROPE_DIM // 2)
    tabs_r = _rope_tables(s, inv_r, HEAD_DIM // 2)
    ret_tables = _retention_tables()
    scale = HEAD_DIM ** -0.5
    col_scale = jnp.ones((IN_COLS,), F32).at[0:WIDTH].set(scale).at[4 * WIDTH:5 * WIDTH].set(scale)

    for l in range(depth):
        w_in_l = (w_in[l] * col_scale[None, :]).astype(BF16)
        p = _inproj(x, ln1[l], w_in_l, tabs_a, tabs_r)
        ya_lo, ya_hi = _moba(p)
        yr = _retention(p, gn_gain[l], ret_tables)
        wo = w_out[l].astype(BF16)
        x = _ffn(x, ya_lo, ya_hi, yr, wo[:WIDTH], wo[WIDTH:], ln2[l],
                 w_up[l][:, :D_FF].astype(BF16), w_up[l][:, D_FF:].astype(BF16),
                 conv_w[l], conv_b[l], w_down[l].astype(BF16),
                 ln_f, final_norm=(l == depth - 1))
    return x
```

```python
import functools

import jax
import jax.numpy as jnp
from jax import lax
from jax.experimental import pallas as pl
from jax.experimental.pallas import tpu as pltpu

D_MODEL = 1024
HEAD_DIM = 64
MOBA_HEADS = 8
RET_HEADS = 8
WIDTH = 512
IN_COLS = 7 * WIDTH
MOBA_BLOCK = 256
MOBA_TOPK = 3
ROPE_THETA = 500000.0
ROPE_DIM = HEAD_DIM // 4
RET_ROPE_THETA = 10000.0
RET_CHUNK = 256
D_FF = 2816
NORM_EPS = 1e-6
GN_EPS = 1e-5
NEG_BIG = -1e9

LANES = 128
HEAD_PAIRS = WIDTH // LANES
SUBLANES = 8
VMEM_LIMIT = 56 * 1024 * 1024

IN_TILE = 512
FFN_TILE = 512
FFN_CHUNK = 256
FFN_NCHUNK = D_FF // FFN_CHUNK
FFN_SLOTS = 6
RET_STEP_CHUNKS = 8
MOBA_PAIRS = 4
LOG2_E = 1.4426950408889634

F32 = jnp.float32
BF16 = jnp.bfloat16
NT_DIMS = (((1,), (1,)), ((), ()))
TN_DIMS = (((0,), (0,)), ((), ()))


def _rmsnorm(x, g):
    return x * lax.rsqrt(jnp.mean(x * x, axis=-1, keepdims=True) + NORM_EPS) * g


def _resident(shape):
    zeros = (0,) * len(shape)
    return pl.BlockSpec(shape, lambda *_: zeros, pipeline_mode=pl.Buffered(1))


def _rope_tables(seq, inv_freq, half):
    ang = jnp.arange(seq, dtype=F32)[:, None] * inv_freq[None, :]
    cos, sin = jnp.cos(ang), jnp.sin(ang)
    pad = HEAD_DIM - 2 * half
    ones = jnp.ones((seq, pad), F32)
    zeros = jnp.zeros((seq, pad), F32)
    zh = jnp.zeros((seq, half), F32)
    c = jnp.concatenate([cos, cos, ones], axis=1)
    su = jnp.concatenate([-sin, zh, zeros], axis=1)
    sd = jnp.concatenate([zh, sin, zeros], axis=1)
    two = lambda t: jnp.concatenate([t, t], axis=1)
    return two(c), two(su), two(sd)


def _inproj_kernel(x_ref, ln_ref, w_ref, ca_ref, sau_ref, sad_ref, cr_ref, sru_ref, srd_ref,
                   p_ref):
    h = _rmsnorm(x_ref[0], ln_ref[...]).astype(BF16)

    def rope(y, c, su, sd, half):
        outs = []
        for g in range(WIDTH // LANES):
            yg = y[:, g * LANES:(g + 1) * LANES]
            up = pltpu.roll(yg, LANES - half, 1)
            dn = pltpu.roll(yg, half, 1)
            outs.append(yg * c + up * su + dn * sd)
        return jnp.concatenate(outs, axis=1)

    for slab in range(IN_COLS // WIDTH):
        cols = slice(slab * WIDTH, (slab + 1) * WIDTH)
        y = jnp.dot(h, w_ref[:, cols], preferred_element_type=F32)
        if slab in (0, 1):
            y = rope(y, ca_ref[...], sau_ref[...], sad_ref[...], ROPE_DIM // 2)
        elif slab in (3, 4):
            y = rope(y, cr_ref[...], sru_ref[...], srd_ref[...], HEAD_DIM // 2)
        p_ref[0, :, cols] = y.astype(BF16)


def _inproj(x, ln, w_bf16, tabs_a, tabs_r):
    b, s, _ = x.shape
    tm = IN_TILE
    tab = pl.BlockSpec((tm, LANES), lambda si, bi: (si, 0))
    return pl.pallas_call(
        _inproj_kernel,
        grid=(s // tm, b),
        in_specs=[
            pl.BlockSpec((1, tm, D_MODEL), lambda si, bi: (bi, si, 0)),
            _resident((1, D_MODEL)),
            _resident((D_MODEL, IN_COLS)),
            tab, tab, tab, tab, tab, tab,
        ],
        out_specs=pl.BlockSpec((1, tm, IN_COLS), lambda si, bi: (bi, si, 0)),
        out_shape=jax.ShapeDtypeStruct((b, s, IN_COLS), BF16),
        compiler_params=pltpu.CompilerParams(
            dimension_semantics=("arbitrary", "arbitrary"), vmem_limit_bytes=VMEM_LIMIT),
        name="inproj",
    )(x, ln.reshape(1, D_MODEL), w_bf16, *tabs_a, *tabs_r)


def _moba_kernel(qlo_ref, qhi_ref, qlon_ref, qhin_ref, k_ref, v_ref, olo_ref, ohi_ref,
                 kaug_ref, vaug_ref, kbar_ref, qaug_ref, qnext_ref, s_ref, mx_ref, m_ref, acc_ref,
                 *, nblocks):
    u = pl.program_id(2)
    blk = MOBA_BLOCK
    npast = nblocks - 1
    lane = lax.broadcasted_iota(jnp.int32, (blk, LANES), 1)
    head0 = lane < HEAD_DIM
    lane_b = lane.astype(F32).astype(BF16)
    head0_b = lane_b < HEAD_DIM
    one = jnp.ones((blk, LANES), BF16)
    zero = jnp.zeros((blk, LANES), BF16)

    def build_queries(lo_ref, hi_ref, step):
        kbar = kbar_ref[...]
        kb_hi = kbar.astype(BF16)
        kb_lo = (kbar - kb_hi.astype(F32)).astype(BF16)
        kb = jnp.concatenate([kb_hi, kb_lo], axis=0)
        n_iota = lax.broadcasted_iota(jnp.int32, (nblocks, blk), 0)
        fill = jnp.zeros((HEAD_DIM - nblocks, blk), F32)
        for ps in range(MOBA_PAIRS):
            first = MOBA_PAIRS * step + ps
            sources = ((lo_ref, ps, first), (hi_ref, MOBA_PAIRS - 1 - ps, npast - first))
            for w, (q_ref, half, i) in enumerate(sources):
                q = q_ref[0, half * blk:(half + 1) * blk, :]
                past = n_iota < i
                pens = []
                for e in range(2):
                    qe = jnp.where(head0_b, q, zero) if e == 0 else jnp.where(head0_b, zero, q)
                    g2 = lax.dot_general(kb, qe, NT_DIMS, preferred_element_type=F32)
                    g = jnp.where(past, g2[:nblocks] + g2[nblocks:], NEG_BIG)
                    cnt = jnp.zeros((nblocks, blk), F32)
                    for m in range(nblocks):
                        row = g[m:m + 1, :]
                        ahead = (row > g) | ((row == g) & (n_iota > m))
                        cnt = cnt + jnp.where(ahead, 1.0, 0.0)
                    keep = ((cnt < MOBA_TOPK) & past) | (n_iota == i)
                    pens.append(jnp.where(keep, 0.0, NEG_BIG))
                pen_t = jnp.concatenate([pens[1], fill, pens[0], fill], axis=0)
                pen = pen_t.T.astype(BF16)
                qnext_ref[ps, w, 0] = jnp.where(head0_b, q, pen)
                qnext_ref[ps, w, 1] = jnp.where(head0_b, pen, q)

    @pl.when(u == 0)
    def _():
        for n in range(nblocks):
            rows = slice(n * blk, (n + 1) * blk)
            k = k_ref[0, rows, :]
            v = v_ref[0, rows, :]
            ka0 = jnp.where(head0_b, k, jnp.where(lane_b == HEAD_DIM + n, one, zero))
            ka1 = jnp.where(head0_b, jnp.where(lane_b == n, one, zero), k)
            kaug_ref[0, n] = ka0.astype(F32).T.astype(BF16)
            kaug_ref[1, n] = ka1.astype(F32).T.astype(BF16)
            vaug_ref[0, rows, :] = jnp.where(head0_b, v, one)
            vaug_ref[1, rows, :] = jnp.where(head0_b, one, v)
            kbar_ref[n:n + 1, :] = jnp.sum(k.astype(F32), axis=0, keepdims=True) / blk
        build_queries(qlo_ref, qhi_ref, 0)

    qaug_ref[...] = qnext_ref[...]
    build_queries(qlon_ref, qhin_ref, jnp.minimum(u + 1, pl.num_programs(2) - 1))

    r_iota = lax.broadcasted_iota(jnp.int32, (blk, blk), 0)
    c_iota = lax.broadcasted_iota(jnp.int32, (blk, blk), 1)
    causal = c_iota <= r_iota

    def unit(first, t):
        w = (t >= first).astype(jnp.int32)
        return w, pl.multiple_of((t - w * first) * blk, blk)

    def scores(ps, w, e, start):
        kblk_t = kaug_ref[e, start // blk]
        return LOG2_E * jnp.dot(qaug_ref[ps, w, e], kblk_t, preferred_element_type=F32)

    def fold(s):
        return jnp.maximum(s[:, :LANES], s[:, LANES:])

    def own_start(first, w):
        return pl.multiple_of((first, npast - first)[w] * blk, blk)

    nunits = npast + 2

    def pass1_unit(ps, first, idx):
        if idx < 2:
            for e in range(2):
                s = jnp.where(causal, scores(ps, idx, e, own_start(first, idx)), NEG_BIG)
                s_ref[ps, e, npast + idx] = s.astype(BF16)
                mx_ref[ps, idx, e] = fold(s)
        else:
            w, start = unit(first, idx - 2)
            for e in range(2):
                s = scores(ps, w, e, start)
                s_ref[ps, e, idx - 2] = s.astype(BF16)
                mx_ref[ps, w, e] = jnp.maximum(mx_ref[ps, w, e], fold(s))

    def row_max(ps):
        for w in range(2):
            for e in range(2):
                m = jnp.max(mx_ref[ps, w, e], axis=1, keepdims=True)
                m_ref[ps, w, e] = jnp.broadcast_to(m, (blk, LANES)).astype(BF16)

    def pv(ps, w, e, t, start):
        m = m_ref[ps, w, e]
        prob = jnp.exp2(s_ref[ps, e, t] - jnp.concatenate([m, m], axis=1))
        return jnp.dot(prob, vaug_ref[e, pl.ds(start, blk), :], preferred_element_type=F32)

    def pass2_unit(ps, first, idx):
        if idx < 2:
            for e in range(2):
                acc_ref[ps, idx, e] = pv(ps, idx, e, npast + idx, own_start(first, idx))
        else:
            w, start = unit(first, idx - 2)
            for e in range(2):
                acc_ref[ps, w, e] += pv(ps, w, e, idx - 2, start)

    def finish(ps):
        targets = ((olo_ref, ps), (ohi_ref, MOBA_PAIRS - 1 - ps))
        for w, (o_ref, half) in enumerate(targets):
            a0 = acc_ref[ps, w, 0]
            a1 = acc_ref[ps, w, 1]
            out = jnp.where(head0, a0 / pltpu.roll(a0, HEAD_DIM, 1), a1 / pltpu.roll(a1, HEAD_DIM, 1))
            o_ref[0, half * blk:(half + 1) * blk, :] = out.astype(BF16)

    firsts = [MOBA_PAIRS * u + ps for ps in range(MOBA_PAIRS)]
    for ps in range(MOBA_PAIRS):
        for idx in range(nunits):
            pass1_unit(ps, firsts[ps], idx)
    for ps in range(MOBA_PAIRS):
        row_max(ps)
        for idx in range(nunits):
            pass2_unit(ps, firsts[ps], idx)
    for ps in range(MOBA_PAIRS):
        finish(ps)


def _moba(p):
    b, s, _ = p.shape
    nblocks = s // MOBA_BLOCK
    blk = MOBA_BLOCK
    rows = MOBA_PAIRS * blk
    steps = nblocks // 2 // MOBA_PAIRS
    assert nblocks <= HEAD_DIM and nblocks % (2 * MOBA_PAIRS) == 0
    assert nblocks % SUBLANES == 0
    nxt = lambda i: jnp.minimum(i + 1, steps - 1)
    qspec = lambda row_block: pl.BlockSpec(
        (1, rows, LANES), lambda bi, hp, i: (bi, row_block(i), hp))
    kspec = pl.BlockSpec((1, s, LANES), lambda bi, hp, i: (bi, 0, HEAD_PAIRS + hp))
    vspec = pl.BlockSpec((1, s, LANES), lambda bi, hp, i: (bi, 0, 2 * HEAD_PAIRS + hp))
    half = jax.ShapeDtypeStruct((b, s // 2, WIDTH), BF16)
    last = 2 * steps - 1
    return pl.pallas_call(
        functools.partial(_moba_kernel, nblocks=nblocks),
        grid=(b, HEAD_PAIRS, steps),
        in_specs=[
            qspec(lambda i: i), qspec(lambda i: last - i),
            qspec(nxt), qspec(lambda i: last - nxt(i)),
            kspec, vspec,
        ],
        out_specs=[
            pl.BlockSpec((1, rows, LANES), lambda bi, hp, i: (bi, i, hp)),
            pl.BlockSpec((1, rows, LANES), lambda bi, hp, i: (bi, steps - 1 - i, hp)),
        ],
        out_shape=[half, half],
        scratch_shapes=[
            pltpu.VMEM((2, nblocks, LANES, blk), BF16),
            pltpu.VMEM((2, s, LANES), BF16),
            pltpu.VMEM((nblocks, LANES), F32),
            pltpu.VMEM((MOBA_PAIRS, 2, 2, blk, LANES), BF16),
            pltpu.VMEM((MOBA_PAIRS, 2, 2, blk, LANES), BF16),
            pltpu.VMEM((MOBA_PAIRS, 2, nblocks + 1, blk, blk), BF16),
            pltpu.VMEM((MOBA_PAIRS, 2, 2, blk, LANES), F32),
            pltpu.VMEM((MOBA_PAIRS, 2, 2, blk, LANES), BF16),
            pltpu.VMEM((MOBA_PAIRS, 2, 2, blk, LANES), F32),
        ],
        compiler_params=pltpu.CompilerParams(
            dimension_semantics=("arbitrary", "arbitrary", "arbitrary"),
            vmem_limit_bytes=VMEM_LIMIT),
        name="moba",
    )(p, p, p, p, p, p)


def _retention_tables():
    c = RET_CHUNK
    log_gamma = jnp.log(1.0 - 2.0 ** (-5.0 - jnp.arange(RET_HEADS, dtype=F32)))
    pos = jnp.arange(c, dtype=F32)
    diff = pos[:, None] - pos[None, :]
    inner = jnp.where(diff[None] >= 0,
                      jnp.exp(jnp.maximum(diff, 0.0)[None] * log_gamma[:, None, None]), 0.0)
    cross = jnp.exp((pos + 1.0)[None, :] * log_gamma[:, None])
    sdec = jnp.exp((c - 1.0 - pos)[None, :] * log_gamma[:, None])
    chunk = jnp.exp(c * log_gamma)

    def lanes(t):
        t = jnp.repeat(t[:, :, None], HEAD_DIM, axis=2)
        t = t.reshape(HEAD_PAIRS, 2, c, HEAD_DIM).transpose(0, 2, 1, 3)
        return t.reshape(HEAD_PAIRS, c, LANES)

    same_head = (jnp.arange(LANES)[:, None] // HEAD_DIM) == (jnp.arange(LANES)[None, :] // HEAD_DIM)
    bd = same_head.astype(F32)
    kdec = jnp.repeat(chunk.reshape(HEAD_PAIRS, 2), HEAD_DIM, axis=1)[:, :, None] * bd[None]
    return inner, lanes(cross), lanes(sdec), kdec, bd


def _retention_kernel(q_ref, k_ref, v_ref, g_ref, idec_ref, cdec_ref, sdec_ref, kdec_ref, bd_ref,
                      gain_ref, o_ref, state_ref):
    lane = lax.broadcasted_iota(jnp.int32, (RET_CHUNK, LANES), 1)
    head0 = lane < HEAD_DIM
    head0_b = lane.astype(F32).astype(BF16) < HEAD_DIM
    zero = jnp.zeros((RET_CHUNK, LANES), BF16)

    @pl.when(pl.program_id(2) == 0)
    def _():
        state_ref[...] = jnp.zeros_like(state_ref)

    def head_mean(t):
        s0 = jnp.sum(jnp.where(head0, t, 0.0), axis=1, keepdims=True)
        s1 = jnp.sum(jnp.where(head0, 0.0, t), axis=1, keepdims=True)
        return jnp.where(head0, s0, s1) / HEAD_DIM

    state = state_ref[...]
    for j in range(RET_STEP_CHUNKS):
        rows = slice(j * RET_CHUNK, (j + 1) * RET_CHUNK)
        q = q_ref[0, rows, :]
        k = k_ref[0, rows, :]
        v = v_ref[0, rows, :]
        o_cross = jnp.dot(q, state.astype(BF16), preferred_element_type=F32) * cdec_ref[0]
        parts = []
        for e in range(2):
            qe = jnp.where(head0_b, q, zero) if e == 0 else jnp.where(head0_b, zero, q)
            s = lax.dot_general(qe, k, NT_DIMS, preferred_element_type=F32) * idec_ref[e]
            parts.append(jnp.dot(s.astype(BF16), v, preferred_element_type=F32))
        o = jnp.where(head0, parts[0], parts[1]) + o_cross

        kd = (k.astype(F32) * sdec_ref[0]).astype(BF16)
        upd = lax.dot_general(kd, v, TN_DIMS, preferred_element_type=F32)
        state = state * kdec_ref[0] + upd * bd_ref[...]

        d = o - head_mean(o)
        on = d * lax.rsqrt(head_mean(d * d) + GN_EPS)
        g = g_ref[0, rows, :].astype(F32)
        y = g * (1.0 / (1.0 + jnp.exp(-g))) * on * gain_ref[...]
        o_ref[0, rows, :] = y.astype(BF16)
    state_ref[...] = state


def _retention(p, gn_gain, tables):
    b, s, _ = p.shape
    c = RET_CHUNK
    inner, cross, sdec, kdec, bd = tables

    rows = RET_STEP_CHUNKS * c

    def col(group):
        return pl.BlockSpec((1, rows, LANES), lambda bi, hp, ci: (bi, ci, group * HEAD_PAIRS + hp))

    pair = lambda shape: pl.BlockSpec(shape, lambda bi, hp, ci: (hp, 0, 0))
    return pl.pallas_call(
        _retention_kernel,
        grid=(b, HEAD_PAIRS, s // rows),
        in_specs=[
            col(3), col(4), col(5), col(6),
            pl.BlockSpec((2, c, c), lambda bi, hp, ci: (hp, 0, 0)),
            pair((1, c, LANES)), pair((1, c, LANES)), pair((1, LANES, LANES)),
            _resident((LANES, LANES)),
            pl.BlockSpec((1, LANES), lambda bi, hp, ci: (0, hp)),
        ],
        out_specs=pl.BlockSpec((1, rows, LANES), lambda bi, hp, ci: (bi, ci, hp)),
        out_shape=jax.ShapeDtypeStruct((b, s, WIDTH), BF16),
        scratch_shapes=[pltpu.VMEM((LANES, LANES), F32)],
        compiler_params=pltpu.CompilerParams(
            dimension_semantics=("arbitrary", "arbitrary", "arbitrary"),
            vmem_limit_bytes=VMEM_LIMIT),
        name="retention",
    )(p, p, p, p, inner, cross, sdec, kdec, bd, gn_gain.reshape(1, WIDTH))


def _ffn_kernel(x_ref, ya_lo_ref, ya_hi_ref, yr_ref, woa_ref, wor_ref, ln2_ref, wg_ref, wu_ref,
                cw_ref, cb_ref, wd_ref, lnf_ref, o_ref, x1_ref, h2_ref, acc_ref, carry_ref, gu_ref,
                *, final_norm):
    tm = x_ref.shape[1]

    @pl.when(pl.program_id(1) == 0)
    def _():
        carry_ref[...] = jnp.zeros_like(carry_ref)

    first_half = (pl.program_id(1) < pl.num_programs(1) // 2).astype(F32)
    pick_lo = jnp.full((tm, WIDTH), first_half, F32).astype(BF16)
    ya = ya_lo_ref[0] * pick_lo + ya_hi_ref[0] * (1 - pick_lo)
    x1 = (x_ref[0]
          + jnp.dot(ya, woa_ref[...], preferred_element_type=F32)
          + jnp.dot(yr_ref[0], wor_ref[...], preferred_element_type=F32))
    x1_ref[...] = x1
    h2_ref[...] = _rmsnorm(x1, ln2_ref[...]).astype(BF16)
    rows = lax.broadcasted_iota(jnp.int32, (tm, FFN_CHUNK), 0)
    row0 = rows == 0
    row1 = rows == 1

    def up_proj(c):
        h2 = h2_ref[...]
        cols = slice(c * FFN_CHUNK, (c + 1) * FFN_CHUNK)
        gu_ref[c % FFN_SLOTS, 0] = jnp.dot(h2, wg_ref[:, cols], preferred_element_type=F32)
        gu_ref[c % FFN_SLOTS, 1] = jnp.dot(h2, wu_ref[:, cols], preferred_element_type=F32)

    def activation(c):
        g = gu_ref[c % FFN_SLOTS, 0]
        u = gu_ref[c % FFN_SLOTS, 1]
        prev = carry_ref[c]
        carry_ref[c] = g[tm - SUBLANES:, :]
        p1 = prev[SUBLANES - 1:SUBLANES, :]
        p2 = prev[SUBLANES - 2:SUBLANES - 1, :]
        g1 = jnp.where(row0, p1, pltpu.roll(g, 1, 0))
        g2 = jnp.where(row0, p2, jnp.where(row1, p1, pltpu.roll(g, 2, 0)))
        cols = slice(c * FFN_CHUNK, (c + 1) * FFN_CHUNK)
        cw = cw_ref[:, cols]
        gc = cw[0:1, :] * g2 + cw[1:2, :] * g1 + cw[2:3, :] * g + cb_ref[:, cols]
        return (gc * (1.0 / (1.0 + jnp.exp(-gc))) * u).astype(BF16)

    for c in range(min(FFN_SLOTS, FFN_NCHUNK)):
        up_proj(c)
    for c0 in range(0, FFN_NCHUNK, 2):
        part = None
        for c in range(c0, min(c0 + 2, FFN_NCHUNK)):
            act = activation(c)
            if c + FFN_SLOTS < FFN_NCHUNK:
                up_proj(c + FFN_SLOTS)
            down = jnp.dot(act, wd_ref[c * FFN_CHUNK:(c + 1) * FFN_CHUNK, :],
                           preferred_element_type=F32)
            part = down if part is None else part + down
        acc_ref[...] = part if c0 == 0 else acc_ref[...] + part
    x2 = x1_ref[...] + acc_ref[...]
    if final_norm:
        x2 = _rmsnorm(x2, lnf_ref[...])
    o_ref[0] = x2


def _ffn(x, ya_lo, ya_hi, yr, woa, wor, ln2, wg, wu, cw, cb, wd, ln_f, final_norm):
    b, s, _ = x.shape
    tm = FFN_TILE
    nhalf = s // tm // 2
    tok = lambda width: pl.BlockSpec((1, tm, width), lambda bi, ti: (bi, ti, 0))
    lo = pl.BlockSpec((1, tm, WIDTH), lambda bi, ti: (bi, jnp.minimum(ti, nhalf - 1), 0))
    hi = pl.BlockSpec((1, tm, WIDTH), lambda bi, ti: (bi, jnp.maximum(ti - nhalf, 0), 0))
    return pl.pallas_call(
        functools.partial(_ffn_kernel, final_norm=final_norm),
        grid=(b, s // tm),
        in_specs=[
            tok(D_MODEL), lo, hi, tok(WIDTH),
            _resident((WIDTH, D_MODEL)), _resident((WIDTH, D_MODEL)),
            _resident((1, D_MODEL)),
            _resident((D_MODEL, D_FF)), _resident((D_MODEL, D_FF)),
            _resident((3, D_FF)), _resident((1, D_FF)),
            _resident((D_FF, D_MODEL)),
            _resident((1, D_MODEL)),
        ],
        out_specs=tok(D_MODEL),
        out_shape=jax.ShapeDtypeStruct((b, s, D_MODEL), F32),
        scratch_shapes=[
            pltpu.VMEM((tm, D_MODEL), F32),
            pltpu.VMEM((tm, D_MODEL), BF16),
            pltpu.VMEM((tm, D_MODEL), F32),
            pltpu.VMEM((FFN_NCHUNK, SUBLANES, FFN_CHUNK), F32),
            pltpu.VMEM((FFN_SLOTS, 2, tm, FFN_CHUNK), F32),
        ],
        compiler_params=pltpu.CompilerParams(
            dimension_semantics=("arbitrary", "arbitrary"), vmem_limit_bytes=VMEM_LIMIT),
        name="ffn",
    )(x, ya_lo, ya_hi, yr, woa, wor, ln2.reshape(1, D_MODEL), wg, wu, cw, cb.reshape(1, D_FF),
      wd,
      ln_f.reshape(1, D_MODEL))


def kernel(x, ln1, w_in, gn_gain, w_out, ln2, w_up, conv_w, conv_b, w_down, ln_f):
    b, s, d = x.shape
    depth = w_in.shape[0]
    assert d == D_MODEL
    assert s % max(IN_TILE, 2 * FFN_TILE, MOBA_BLOCK, RET_STEP_CHUNKS * RET_CHUNK) == 0

    inv_a = ROPE_THETA ** (-jnp.arange(ROPE_DIM // 2, dtype=F32) / (ROPE_DIM // 2))
    inv_r = 1.0 / (RET_ROPE_THETA ** jnp.linspace(0.0, 1.0, HEAD_DIM // 2, dtype=F32))
    tabs_a = _rope_tables(s, inv_a, ROPE_DIM // 2)
    tabs_r = _rope_tables(s, inv_r, HEAD_DIM // 2)
    ret_tables = _retention_tables()
    scale = HEAD_DIM ** -0.5
    col_scale = jnp.ones((IN_COLS,), F32).at[0:WIDTH].set(scale).at[4 * WIDTH:5 * WIDTH].set(scale)

    for l in range(depth):
        w_in_l = (w_in[l] * col_scale[None, :]).astype(BF16)
        p = _inproj(x, ln1[l], w_in_l, tabs_a, tabs_r)
        ya_lo, ya_hi = _moba(p)
        yr = _retention(p, gn_gain[l], ret_tables)
        wo = w_out[l].astype(BF16)
        x = _ffn(x, ya_lo, ya_hi, yr, wo[:WIDTH], wo[WIDTH:], ln2[l],
                 w_up[l][:, :D_FF].astype(BF16), w_up[l][:, D_FF:].astype(BF16),
                 conv_w[l], conv_b[l], w_down[l].astype(BF16),
                 ln_f, final_norm=(l == depth - 1))
    return x
```

```python
import functools

import jax
import jax.numpy as jnp
from jax import lax
from jax.experimental import pallas as pl
from jax.experimental.pallas import tpu as pltpu

D_MODEL = 1024
HEAD_DIM = 64
MOBA_HEADS = 8
RET_HEADS = 8
WIDTH = 512
IN_COLS = 7 * WIDTH
MOBA_BLOCK = 256
MOBA_TOPK = 3
ROPE_THETA = 500000.0
ROPE_DIM = HEAD_DIM // 4
RET_ROPE_THETA = 10000.0
RET_CHUNK = 256
D_FF = 2816
NORM_EPS = 1e-6
GN_EPS = 1e-5
NEG_BIG = -1e9

LANES = 128
HEAD_PAIRS = WIDTH // LANES
SUBLANES = 8
VMEM_LIMIT = 56 * 1024 * 1024

IN_TILE = 1024
FFN_TILE = 512
FFN_CHUNK = 256
FFN_NCHUNK = D_FF // FFN_CHUNK
FFN_SLOTS = 6
RET_STEP_CHUNKS = 8
MOBA_PAIRS = 4
LOG2_E = 1.4426950408889634

F32 = jnp.float32
BF16 = jnp.bfloat16
NT_DIMS = (((1,), (1,)), ((), ()))
TN_DIMS = (((0,), (0,)), ((), ()))


def _rmsnorm(x, g):
    return x * lax.rsqrt(jnp.mean(x * x, axis=-1, keepdims=True) + NORM_EPS) * g


def _resident(shape):
    zeros = (0,) * len(shape)
    return pl.BlockSpec(shape, lambda *_: zeros, pipeline_mode=pl.Buffered(1))


def _rope_tables(seq, inv_freq, half):
    ang = jnp.arange(seq, dtype=F32)[:, None] * inv_freq[None, :]
    cos, sin = jnp.cos(ang), jnp.sin(ang)
    pad = HEAD_DIM - 2 * half
    ones = jnp.ones((seq, pad), F32)
    zeros = jnp.zeros((seq, pad), F32)
    zh = jnp.zeros((seq, half), F32)
    c = jnp.concatenate([cos, cos, ones], axis=1)
    su = jnp.concatenate([-sin, zh, zeros], axis=1)
    sd = jnp.concatenate([zh, sin, zeros], axis=1)
    two = lambda t: jnp.concatenate([t, t], axis=1)
    return two(c), two(su), two(sd)


def _inproj_kernel(x_ref, ln_ref, w_ref, ca_ref, sau_ref, sad_ref, cr_ref, sru_ref, srd_ref,
                   p_ref):
    h = _rmsnorm(x_ref[0], ln_ref[...]).astype(BF16)

    def rope(y, c, su, sd, half):
        outs = []
        for g in range(WIDTH // LANES):
            yg = y[:, g * LANES:(g + 1) * LANES]
            up = pltpu.roll(yg, LANES - half, 1)
            dn = pltpu.roll(yg, half, 1)
            outs.append(yg * c + up * su + dn * sd)
        return jnp.concatenate(outs, axis=1)

    for slab in range(IN_COLS // WIDTH):
        cols = slice(slab * WIDTH, (slab + 1) * WIDTH)
        y = jnp.dot(h, w_ref[:, cols], preferred_element_type=F32)
        if slab in (0, 1):
            y = rope(y, ca_ref[...], sau_ref[...], sad_ref[...], ROPE_DIM // 2)
        elif slab in (3, 4):
            y = rope(y, cr_ref[...], sru_ref[...], srd_ref[...], HEAD_DIM // 2)
        p_ref[0, :, cols] = y.astype(BF16)


def _inproj(x, ln, w_bf16, tabs_a, tabs_r):
    b, s, _ = x.shape
    tm = IN_TILE
    tab = pl.BlockSpec((tm, LANES), lambda si, bi: (si, 0))
    return pl.pallas_call(
        _inproj_kernel,
        grid=(s // tm, b),
        in_specs=[
            pl.BlockSpec((1, tm, D_MODEL), lambda si, bi: (bi, si, 0)),
            _resident((1, D_MODEL)),
            _resident((D_MODEL, IN_COLS)),
            tab, tab, tab, tab, tab, tab,
        ],
        out_specs=pl.BlockSpec((1, tm, IN_COLS), lambda si, bi: (bi, si, 0)),
        out_shape=jax.ShapeDtypeStruct((b, s, IN_COLS), BF16),
        compiler_params=pltpu.CompilerParams(
            dimension_semantics=("arbitrary", "arbitrary"), vmem_limit_bytes=VMEM_LIMIT),
        name="inproj",
    )(x, ln.reshape(1, D_MODEL), w_bf16, *tabs_a, *tabs_r)


def _moba_kernel(qlo_ref, qhi_ref, qlon_ref, qhin_ref, k_ref, v_ref, olo_ref, ohi_ref,
                 kaug_ref, vaug_ref, kbar_ref, qaug_ref, qnext_ref, s_ref, mx_ref, m_ref, acc_ref,
                 *, nblocks):
    u = pl.program_id(2)
    blk = MOBA_BLOCK
    npast = nblocks - 1
    lane = lax.broadcasted_iota(jnp.int32, (blk, LANES), 1)
    head0 = lane < HEAD_DIM
    lane_b = lane.astype(F32).astype(BF16)
    head0_b = lane_b < HEAD_DIM
    one = jnp.ones((blk, LANES), BF16)
    zero = jnp.zeros((blk, LANES), BF16)

    def build_queries(lo_ref, hi_ref, step):
        kbar = kbar_ref[...]
        kb_hi = kbar.astype(BF16)
        kb_lo = (kbar - kb_hi.astype(F32)).astype(BF16)
        kb = jnp.concatenate([kb_hi, kb_lo], axis=0)
        n_iota = lax.broadcasted_iota(jnp.int32, (nblocks, blk), 0)
        fill = jnp.zeros((HEAD_DIM - nblocks, blk), F32)
        for ps in range(MOBA_PAIRS):
            first = MOBA_PAIRS * step + ps
            sources = ((lo_ref, ps, first), (hi_ref, MOBA_PAIRS - 1 - ps, npast - first))
            for w, (q_ref, half, i) in enumerate(sources):
                q = q_ref[0, half * blk:(half + 1) * blk, :]
                past = n_iota < i
                pens = []
                for e in range(2):
                    qe = jnp.where(head0_b, q, zero) if e == 0 else jnp.where(head0_b, zero, q)
                    g2 = lax.dot_general(kb, qe, NT_DIMS, preferred_element_type=F32)
                    g = jnp.where(past, g2[:nblocks] + g2[nblocks:], NEG_BIG)
                    cnt = jnp.zeros((nblocks, blk), F32)
                    for m in range(nblocks):
                        row = g[m:m + 1, :]
                        ahead = (row > g) | ((row == g) & (n_iota > m))
                        cnt = cnt + jnp.where(ahead, 1.0, 0.0)
                    keep = ((cnt < MOBA_TOPK) & past) | (n_iota == i)
                    pens.append(jnp.where(keep, 0.0, NEG_BIG))
                pen_t = jnp.concatenate([pens[1], fill, pens[0], fill], axis=0)
                pen = pen_t.T.astype(BF16)
                qnext_ref[ps, w, 0] = jnp.where(head0_b, q, pen)
                qnext_ref[ps, w, 1] = jnp.where(head0_b, pen, q)

    @pl.when(u == 0)
    def _():
        for n in range(nblocks):
            rows = slice(n * blk, (n + 1) * blk)
            k = k_ref[0, rows, :]
            v = v_ref[0, rows, :]
            kaug_ref[0, rows, :] = jnp.where(head0_b, k, jnp.where(lane_b == HEAD_DIM + n, one, zero))
            kaug_ref[1, rows, :] = jnp.where(head0_b, jnp.where(lane_b == n, one, zero), k)
            vaug_ref[0, rows, :] = jnp.where(head0_b, v, one)
            vaug_ref[1, rows, :] = jnp.where(head0_b, one, v)
            kbar_ref[n:n + 1, :] = jnp.sum(k.astype(F32), axis=0, keepdims=True) / blk
        build_queries(qlo_ref, qhi_ref, 0)

    qaug_ref[...] = qnext_ref[...]
    build_queries(qlon_ref, qhin_ref, jnp.minimum(u + 1, pl.num_programs(2) - 1))

    r_iota = lax.broadcasted_iota(jnp.int32, (blk, blk), 0)
    c_iota = lax.broadcasted_iota(jnp.int32, (blk, blk), 1)
    causal = c_iota <= r_iota

    def unit(first, t):
        w = (t >= first).astype(jnp.int32)
        return w, pl.multiple_of((t - w * first) * blk, blk)

    def scores(ps, w, e, start):
        kblk = kaug_ref[e, pl.ds(start, blk), :]
        return LOG2_E * lax.dot_general(qaug_ref[ps, w, e], kblk, NT_DIMS,
                                        preferred_element_type=F32)

    def fold(s):
        return jnp.maximum(s[:, :LANES], s[:, LANES:])

    def own_start(first, w):
        return pl.multiple_of((first, npast - first)[w] * blk, blk)

    nunits = npast + 2

    def pass1_unit(ps, first, idx):
        if idx < 2:
            for e in range(2):
                s = jnp.where(causal, scores(ps, idx, e, own_start(first, idx)), NEG_BIG)
                s_ref[ps, e, npast + idx] = s
                mx_ref[ps, idx, e] = fold(s)
        else:
            w, start = unit(first, idx - 2)
            for e in range(2):
                s = scores(ps, w, e, start)
                s_ref[ps, e, idx - 2] = s
                mx_ref[ps, w, e] = jnp.maximum(mx_ref[ps, w, e], fold(s))

    def row_max(ps):
        for w in range(2):
            for e in range(2):
                m = jnp.max(mx_ref[ps, w, e], axis=1, keepdims=True)
                m_ref[ps, w, e] = jnp.broadcast_to(m, (blk, LANES))

    def pv(ps, w, e, t, start):
        m = m_ref[ps, w, e]
        prob = jnp.exp2(s_ref[ps, e, t] - jnp.concatenate([m, m], axis=1)).astype(BF16)
        return jnp.dot(prob, vaug_ref[e, pl.ds(start, blk), :], preferred_element_type=F32)

    def pass2_unit(ps, first, idx):
        if idx < 2:
            for e in range(2):
                acc_ref[ps, idx, e] = pv(ps, idx, e, npast + idx, own_start(first, idx))
        else:
            w, start = unit(first, idx - 2)
            for e in range(2):
                acc_ref[ps, w, e] += pv(ps, w, e, idx - 2, start)

    def finish(ps):
        targets = ((olo_ref, ps), (ohi_ref, MOBA_PAIRS - 1 - ps))
        for w, (o_ref, half) in enumerate(targets):
            a0 = acc_ref[ps, w, 0]
            a1 = acc_ref[ps, w, 1]
            out = jnp.where(head0, a0 / pltpu.roll(a0, HEAD_DIM, 1), a1 / pltpu.roll(a1, HEAD_DIM, 1))
            o_ref[0, half * blk:(half + 1) * blk, :] = out.astype(BF16)

    firsts = [MOBA_PAIRS * u + ps for ps in range(MOBA_PAIRS)]
    for ps in range(MOBA_PAIRS):
        for idx in range(nunits):
            pass1_unit(ps, firsts[ps], idx)
    for ps in range(MOBA_PAIRS):
        row_max(ps)
        for idx in range(nunits):
            pass2_unit(ps, firsts[ps], idx)
    for ps in range(MOBA_PAIRS):
        finish(ps)


def _moba(p):
    b, s, _ = p.shape
    nblocks = s // MOBA_BLOCK
    blk = MOBA_BLOCK
    rows = MOBA_PAIRS * blk
    steps = nblocks // 2 // MOBA_PAIRS
    assert nblocks <= HEAD_DIM and nblocks % (2 * MOBA_PAIRS) == 0
    assert nblocks % SUBLANES == 0
    nxt = lambda i: jnp.minimum(i + 1, steps - 1)
    qspec = lambda row_block: pl.BlockSpec(
        (1, rows, LANES), lambda bi, hp, i: (bi, row_block(i), hp))
    kspec = pl.BlockSpec((1, s, LANES), lambda bi, hp, i: (bi, 0, HEAD_PAIRS + hp))
    vspec = pl.BlockSpec((1, s, LANES), lambda bi, hp, i: (bi, 0, 2 * HEAD_PAIRS + hp))
    half = jax.ShapeDtypeStruct((b, s // 2, WIDTH), BF16)
    last = 2 * steps - 1
    return pl.pallas_call(
        functools.partial(_moba_kernel, nblocks=nblocks),
        grid=(b, HEAD_PAIRS, steps),
        in_specs=[
            qspec(lambda i: i), qspec(lambda i: last - i),
            qspec(nxt), qspec(lambda i: last - nxt(i)),
            kspec, vspec,
        ],
        out_specs=[
            pl.BlockSpec((1, rows, LANES), lambda bi, hp, i: (bi, i, hp)),
            pl.BlockSpec((1, rows, LANES), lambda bi, hp, i: (bi, steps - 1 - i, hp)),
        ],
        out_shape=[half, half],
        scratch_shapes=[
            pltpu.VMEM((2, s, LANES), BF16),
            pltpu.VMEM((2, s, LANES), BF16),
            pltpu.VMEM((nblocks, LANES), F32),
            pltpu.VMEM((MOBA_PAIRS, 2, 2, blk, LANES), BF16),
            pltpu.VMEM((MOBA_PAIRS, 2, 2, blk, LANES), BF16),
            pltpu.VMEM((MOBA_PAIRS, 2, nblocks + 1, blk, blk), F32),
            pltpu.VMEM((MOBA_PAIRS, 2, 2, blk, LANES), F32),
            pltpu.VMEM((MOBA_PAIRS, 2, 2, blk, LANES), F32),
            pltpu.VMEM((MOBA_PAIRS, 2, 2, blk, LANES), F32),
        ],
        compiler_params=pltpu.CompilerParams(
            dimension_semantics=("arbitrary", "arbitrary", "arbitrary"),
            vmem_limit_bytes=VMEM_LIMIT),
        name="moba",
    )(p, p, p, p, p, p)


def _retention_tables():
    c = RET_CHUNK
    log_gamma = jnp.log(1.0 - 2.0 ** (-5.0 - jnp.arange(RET_HEADS, dtype=F32)))
    pos = jnp.arange(c, dtype=F32)
    diff = pos[:, None] - pos[None, :]
    inner = jnp.where(diff[None] >= 0,
                      jnp.exp(jnp.maximum(diff, 0.0)[None] * log_gamma[:, None, None]), 0.0)
    cross = jnp.exp((pos + 1.0)[None, :] * log_gamma[:, None])
    sdec = jnp.exp((c - 1.0 - pos)[None, :] * log_gamma[:, None])
    chunk = jnp.exp(c * log_gamma)

    def lanes(t):
        t = jnp.repeat(t[:, :, None], HEAD_DIM, axis=2)
        t = t.reshape(HEAD_PAIRS, 2, c, HEAD_DIM).transpose(0, 2, 1, 3)
        return t.reshape(HEAD_PAIRS, c, LANES)

    same_head = (jnp.arange(LANES)[:, None] // HEAD_DIM) == (jnp.arange(LANES)[None, :] // HEAD_DIM)
    bd = same_head.astype(F32)
    kdec = jnp.repeat(chunk.reshape(HEAD_PAIRS, 2), HEAD_DIM, axis=1)[:, :, None] * bd[None]
    return inner, lanes(cross), lanes(sdec), kdec, bd


def _retention_kernel(q_ref, k_ref, v_ref, g_ref, idec_ref, cdec_ref, sdec_ref, kdec_ref, bd_ref,
                      gain_ref, o_ref, state_ref):
    lane = lax.broadcasted_iota(jnp.int32, (RET_CHUNK, LANES), 1)
    head0 = lane < HEAD_DIM
    head0_b = lane.astype(F32).astype(BF16) < HEAD_DIM
    zero = jnp.zeros((RET_CHUNK, LANES), BF16)

    @pl.when(pl.program_id(2) == 0)
    def _():
        state_ref[...] = jnp.zeros_like(state_ref)

    def head_mean(t):
        s0 = jnp.sum(jnp.where(head0, t, 0.0), axis=1, keepdims=True)
        s1 = jnp.sum(jnp.where(head0, 0.0, t), axis=1, keepdims=True)
        return jnp.where(head0, s0, s1) / HEAD_DIM

    state = state_ref[...]
    for j in range(RET_STEP_CHUNKS):
        rows = slice(j * RET_CHUNK, (j + 1) * RET_CHUNK)
        q = q_ref[0, rows, :]
        k = k_ref[0, rows, :]
        v = v_ref[0, rows, :]
        o_cross = jnp.dot(q, state.astype(BF16), preferred_element_type=F32) * cdec_ref[0]
        parts = []
        for e in range(2):
            qe = jnp.where(head0_b, q, zero) if e == 0 else jnp.where(head0_b, zero, q)
            s = lax.dot_general(qe, k, NT_DIMS, preferred_element_type=F32) * idec_ref[e]
            parts.append(jnp.dot(s.astype(BF16), v, preferred_element_type=F32))
        o = jnp.where(head0, parts[0], parts[1]) + o_cross

        kd = (k.astype(F32) * sdec_ref[0]).astype(BF16)
        upd = lax.dot_general(kd, v, TN_DIMS, preferred_element_type=F32)
        state = state * kdec_ref[0] + upd * bd_ref[...]

        d = o - head_mean(o)
        on = d * lax.rsqrt(head_mean(d * d) + GN_EPS)
        g = g_ref[0, rows, :].astype(F32)
        y = g * (1.0 / (1.0 + jnp.exp(-g))) * on * gain_ref[...]
        o_ref[0, rows, :] = y.astype(BF16)
    state_ref[...] = state


def _retention(p, gn_gain, tables):
    b, s, _ = p.shape
    c = RET_CHUNK
    inner, cross, sdec, kdec, bd = tables

    rows = RET_STEP_CHUNKS * c

    def col(group):
        return pl.BlockSpec((1, rows, LANES), lambda bi, hp, ci: (bi, ci, group * HEAD_PAIRS + hp))

    pair = lambda shape: pl.BlockSpec(shape, lambda bi, hp, ci: (hp, 0, 0))
    return pl.pallas_call(
        _retention_kernel,
        grid=(b, HEAD_PAIRS, s // rows),
        in_specs=[
            col(3), col(4), col(5), col(6),
            pl.BlockSpec((2, c, c), lambda bi, hp, ci: (hp, 0, 0)),
            pair((1, c, LANES)), pair((1, c, LANES)), pair((1, LANES, LANES)),
            _resident((LANES, LANES)),
            pl.BlockSpec((1, LANES), lambda bi, hp, ci: (0, hp)),
        ],
        out_specs=pl.BlockSpec((1, rows, LANES), lambda bi, hp, ci: (bi, ci, hp)),
        out_shape=jax.ShapeDtypeStruct((b, s, WIDTH), BF16),
        scratch_shapes=[pltpu.VMEM((LANES, LANES), F32)],
        compiler_params=pltpu.CompilerParams(
            dimension_semantics=("arbitrary", "arbitrary", "arbitrary"),
            vmem_limit_bytes=VMEM_LIMIT),
        name="retention",
    )(p, p, p, p, inner, cross, sdec, kdec, bd, gn_gain.reshape(1, WIDTH))


def _ffn_kernel(x_ref, ya_lo_ref, ya_hi_ref, yr_ref, woa_ref, wor_ref, ln2_ref, wg_ref, wu_ref,
                cw_ref, cb_ref, wd_ref, lnf_ref, o_ref, x1_ref, h2_ref, acc_ref, carry_ref, gu_ref,
                *, final_norm):
    tm = x_ref.shape[1]

    @pl.when(pl.program_id(1) == 0)
    def _():
        carry_ref[...] = jnp.zeros_like(carry_ref)

    first_half = (pl.program_id(1) < pl.num_programs(1) // 2).astype(F32)
    pick_lo = jnp.full((tm, WIDTH), first_half, F32).astype(BF16)
    ya = ya_lo_ref[0] * pick_lo + ya_hi_ref[0] * (1 - pick_lo)
    x1 = (x_ref[0]
          + jnp.dot(ya, woa_ref[...], preferred_element_type=F32)
          + jnp.dot(yr_ref[0], wor_ref[...], preferred_element_type=F32))
    x1_ref[...] = x1
    h2_ref[...] = _rmsnorm(x1, ln2_ref[...]).astype(BF16)
    rows = lax.broadcasted_iota(jnp.int32, (tm, FFN_CHUNK), 0)
    row0 = rows == 0
    row1 = rows == 1

    def up_proj(c):
        h2 = h2_ref[...]
        cols = slice(c * FFN_CHUNK, (c + 1) * FFN_CHUNK)
        gu_ref[c % FFN_SLOTS, 0] = jnp.dot(h2, wg_ref[:, cols], preferred_element_type=F32)
        gu_ref[c % FFN_SLOTS, 1] = jnp.dot(h2, wu_ref[:, cols], preferred_element_type=F32)

    def activation(c):
        g = gu_ref[c % FFN_SLOTS, 0]
        u = gu_ref[c % FFN_SLOTS, 1]
        prev = carry_ref[c]
        carry_ref[c] = g[tm - SUBLANES:, :]
        p1 = prev[SUBLANES - 1:SUBLANES, :]
        p2 = prev[SUBLANES - 2:SUBLANES - 1, :]
        g1 = jnp.where(row0, p1, pltpu.roll(g, 1, 0))
        g2 = jnp.where(row0, p2, jnp.where(row1, p1, pltpu.roll(g, 2, 0)))
        cols = slice(c * FFN_CHUNK, (c + 1) * FFN_CHUNK)
        cw = cw_ref[:, cols]
        gc = cw[0:1, :] * g2 + cw[1:2, :] * g1 + cw[2:3, :] * g + cb_ref[:, cols]
        return (gc * (1.0 / (1.0 + jnp.exp(-gc))) * u).astype(BF16)

    for c in range(min(FFN_SLOTS, FFN_NCHUNK)):
        up_proj(c)
    for c0 in range(0, FFN_NCHUNK, 2):
        part = None
        for c in range(c0, min(c0 + 2, FFN_NCHUNK)):
            act = activation(c)
            if c + FFN_SLOTS < FFN_NCHUNK:
                up_proj(c + FFN_SLOTS)
            down = jnp.dot(act, wd_ref[c * FFN_CHUNK:(c + 1) * FFN_CHUNK, :],
                           preferred_element_type=F32)
            part = down if part is None else part + down
        acc_ref[...] = part if c0 == 0 else acc_ref[...] + part
    x2 = x1_ref[...] + acc_ref[...]
    if final_norm:
        x2 = _rmsnorm(x2, lnf_ref[...])
    o_ref[0] = x2


def _ffn(x, ya_lo, ya_hi, yr, woa, wor, ln2, wg, wu, cw, cb, wd, ln_f, final_norm):
    b, s, _ = x.shape
    tm = FFN_TILE
    nhalf = s // tm // 2
    tok = lambda width: pl.BlockSpec((1, tm, width), lambda bi, ti: (bi, ti, 0))
    lo = pl.BlockSpec((1, tm, WIDTH), lambda bi, ti: (bi, jnp.minimum(ti, nhalf - 1), 0))
    hi = pl.BlockSpec((1, tm, WIDTH), lambda bi, ti: (bi, jnp.maximum(ti - nhalf, 0), 0))
    return pl.pallas_call(
        functools.partial(_ffn_kernel, final_norm=final_norm),
        grid=(b, s // tm),
        in_specs=[
            tok(D_MODEL), lo, hi, tok(WIDTH),
            _resident((WIDTH, D_MODEL)), _resident((WIDTH, D_MODEL)),
            _resident((1, D_MODEL)),
            _resident((D_MODEL, D_FF)), _resident((D_MODEL, D_FF)),
            _resident((3, D_FF)), _resident((1, D_FF)),
            _resident((D_FF, D_MODEL)),
            _resident((1, D_MODEL)),
        ],
        out_specs=tok(D_MODEL),
        out_shape=jax.ShapeDtypeStruct((b, s, D_MODEL), F32),
        scratch_shapes=[
            pltpu.VMEM((tm, D_MODEL), F32),
            pltpu.VMEM((tm, D_MODEL), BF16),
            pltpu.VMEM((tm, D_MODEL), F32),
            pltpu.VMEM((FFN_NCHUNK, SUBLANES, FFN_CHUNK), F32),
            pltpu.VMEM((FFN_SLOTS, 2, tm, FFN_CHUNK), F32),
        ],
        compiler_params=pltpu.CompilerParams(
            dimension_semantics=("arbitrary", "arbitrary"), vmem_limit_bytes=VMEM_LIMIT),
        name="ffn",
    )(x, ya_lo, ya_hi, yr, woa, wor, ln2.reshape(1, D_MODEL), wg, wu, cw, cb.reshape(1, D_FF),
      wd,
      ln_f.reshape(1, D_MODEL))


def kernel(x, ln1, w_in, gn_gain, w_out, ln2, w_up, conv_w, conv_b, w_down, ln_f):
    b, s, d = x.shape
    depth = w_in.shape[0]
    assert d == D_MODEL
    assert s % max(IN_TILE, 2 * FFN_TILE, MOBA_BLOCK, RET_STEP_CHUNKS * RET_CHUNK) == 0

    inv_a = ROPE_THETA ** (-jnp.arange(ROPE_DIM // 2, dtype=F32) / (ROPE_DIM // 2))
    inv_r = 1.0 / (RET_ROPE_THETA ** jnp.linspace(0.0, 1.0, HEAD_DIM // 2, dtype=F32))
    tabs_a = _rope_tables(s, inv_a, ROPE_DIM // 2)
    tabs_r = _rope_tables(s, inv_r, HEAD_DIM // 2)
    ret_tables = _retention_tables()
    scale = HEAD_DIM ** -0.5
    col_scale = jnp.ones((IN_COLS,), F32).at[0:WIDTH].set(scale).at[4 * WIDTH:5 * WIDTH].set(scale)

    for l in range(depth):
        w_in_l = (w_in[l] * col_scale[None, :]).astype(BF16)
        p = _inproj(x, ln1[l], w_in_l, tabs_a, tabs_r)
        ya_lo, ya_hi = _moba(p)
        yr = _retention(p, gn_gain[l], ret_tables)
        wo = w_out[l].astype(BF16)
        x = _ffn(x, ya_lo, ya_hi, yr, wo[:WIDTH], wo[WIDTH:], ln2[l],
                 w_up[l][:, :D_FF].astype(BF16), w_up[l][:, D_FF:].astype(BF16),
                 conv_w[l], conv_b[l], w_down[l].astype(BF16),
                 ln_f, final_norm=(l == depth - 1))
    return x
```

```python
import functools

import jax
import jax.numpy as jnp
from jax import lax
from jax.experimental import pallas as pl
from jax.experimental.pallas import tpu as pltpu

D_MODEL = 1024
HEAD_DIM = 64
MOBA_HEADS = 8
RET_HEADS = 8
WIDTH = 512
IN_COLS = 7 * WIDTH
MOBA_BLOCK = 256
MOBA_TOPK = 3
ROPE_THETA = 500000.0
ROPE_DIM = HEAD_DIM // 4
RET_ROPE_THETA = 10000.0
RET_CHUNK = 256
D_FF = 2816
NORM_EPS = 1e-6
GN_EPS = 1e-5
NEG_BIG = -1e9

LANES = 128
HEAD_PAIRS = WIDTH // LANES
SUBLANES = 8
VMEM_LIMIT = 56 * 1024 * 1024

IN_TILE = 1024
FFN_TILE = 512
FFN_CHUNK = 256
FFN_NCHUNK = D_FF // FFN_CHUNK
FFN_SLOTS = 6
RET_STEP_CHUNKS = 16
MOBA_PAIRS = 4
LOG2_E = 1.4426950408889634

F32 = jnp.float32
BF16 = jnp.bfloat16
NT_DIMS = (((1,), (1,)), ((), ()))
TN_DIMS = (((0,), (0,)), ((), ()))


def _rmsnorm(x, g):
    return x * lax.rsqrt(jnp.mean(x * x, axis=-1, keepdims=True) + NORM_EPS) * g


def _resident(shape):
    zeros = (0,) * len(shape)
    return pl.BlockSpec(shape, lambda *_: zeros, pipeline_mode=pl.Buffered(1))


def _rope_tables(seq, inv_freq, half):
    ang = jnp.arange(seq, dtype=F32)[:, None] * inv_freq[None, :]
    cos, sin = jnp.cos(ang), jnp.sin(ang)
    pad = HEAD_DIM - 2 * half
    ones = jnp.ones((seq, pad), F32)
    zeros = jnp.zeros((seq, pad), F32)
    zh = jnp.zeros((seq, half), F32)
    c = jnp.concatenate([cos, cos, ones], axis=1)
    su = jnp.concatenate([-sin, zh, zeros], axis=1)
    sd = jnp.concatenate([zh, sin, zeros], axis=1)
    two = lambda t: jnp.concatenate([t, t], axis=1)
    return two(c), two(su), two(sd)


def _inproj_kernel(x_ref, ln_ref, w_ref, ca_ref, sau_ref, sad_ref, cr_ref, sru_ref, srd_ref,
                   p_ref):
    h = _rmsnorm(x_ref[0], ln_ref[...]).astype(BF16)

    def rope(y, c, su, sd, half):
        outs = []
        for g in range(WIDTH // LANES):
            yg = y[:, g * LANES:(g + 1) * LANES]
            up = pltpu.roll(yg, LANES - half, 1)
            dn = pltpu.roll(yg, half, 1)
            outs.append(yg * c + up * su + dn * sd)
        return jnp.concatenate(outs, axis=1)

    for slab in range(IN_COLS // WIDTH):
        cols = slice(slab * WIDTH, (slab + 1) * WIDTH)
        y = jnp.dot(h, w_ref[:, cols], preferred_element_type=F32)
        if slab in (0, 1):
            y = rope(y, ca_ref[...], sau_ref[...], sad_ref[...], ROPE_DIM // 2)
        elif slab in (3, 4):
            y = rope(y, cr_ref[...], sru_ref[...], srd_ref[...], HEAD_DIM // 2)
        p_ref[0, :, cols] = y.astype(BF16)


def _inproj(x, ln, w_bf16, tabs_a, tabs_r):
    b, s, _ = x.shape
    tm = IN_TILE
    tab = pl.BlockSpec((tm, LANES), lambda si, bi: (si, 0))
    return pl.pallas_call(
        _inproj_kernel,
        grid=(s // tm, b),
        in_specs=[
            pl.BlockSpec((1, tm, D_MODEL), lambda si, bi: (bi, si, 0)),
            _resident((1, D_MODEL)),
            _resident((D_MODEL, IN_COLS)),
            tab, tab, tab, tab, tab, tab,
        ],
        out_specs=pl.BlockSpec((1, tm, IN_COLS), lambda si, bi: (bi, si, 0)),
        out_shape=jax.ShapeDtypeStruct((b, s, IN_COLS), BF16),
        compiler_params=pltpu.CompilerParams(
            dimension_semantics=("arbitrary", "arbitrary"), vmem_limit_bytes=VMEM_LIMIT),
        name="inproj",
    )(x, ln.reshape(1, D_MODEL), w_bf16, *tabs_a, *tabs_r)


def _moba_kernel(qlo_ref, qhi_ref, qlon_ref, qhin_ref, k_ref, v_ref, olo_ref, ohi_ref,
                 kaug_ref, vaug_ref, kbar_ref, qaug_ref, qnext_ref, s_ref, mx_ref, m_ref, acc_ref,
                 *, nblocks):
    u = pl.program_id(2)
    blk = MOBA_BLOCK
    npast = nblocks - 1
    lane = lax.broadcasted_iota(jnp.int32, (blk, LANES), 1)
    head0 = lane < HEAD_DIM
    lane_b = lane.astype(F32).astype(BF16)
    head0_b = lane_b < HEAD_DIM
    one = jnp.ones((blk, LANES), BF16)
    zero = jnp.zeros((blk, LANES), BF16)

    def build_queries(lo_ref, hi_ref, step):
        kbar = kbar_ref[...]
        kb_hi = kbar.astype(BF16)
        kb_lo = (kbar - kb_hi.astype(F32)).astype(BF16)
        kb = jnp.concatenate([kb_hi, kb_lo], axis=0)
        n_iota = lax.broadcasted_iota(jnp.int32, (nblocks, blk), 0)
        fill = jnp.zeros((HEAD_DIM - nblocks, blk), F32)
        for ps in range(MOBA_PAIRS):
            first = MOBA_PAIRS * step + ps
            sources = ((lo_ref, ps, first), (hi_ref, MOBA_PAIRS - 1 - ps, npast - first))
            for w, (q_ref, half, i) in enumerate(sources):
                q = q_ref[0, half * blk:(half + 1) * blk, :]
                past = n_iota < i
                pens = []
                for e in range(2):
                    qe = jnp.where(head0_b, q, zero) if e == 0 else jnp.where(head0_b, zero, q)
                    g2 = lax.dot_general(kb, qe, NT_DIMS, preferred_element_type=F32)
                    g = jnp.where(past, g2[:nblocks] + g2[nblocks:], NEG_BIG)
                    cnt = jnp.zeros((nblocks, blk), F32)
                    for m in range(nblocks):
                        row = g[m:m + 1, :]
                        ahead = (row > g) | ((row == g) & (n_iota > m))
                        cnt = cnt + jnp.where(ahead, 1.0, 0.0)
                    keep = ((cnt < MOBA_TOPK) & past) | (n_iota == i)
                    pens.append(jnp.where(keep, 0.0, NEG_BIG))
                pen_t = jnp.concatenate([pens[1], fill, pens[0], fill], axis=0)
                pen = pen_t.T.astype(BF16)
                qnext_ref[ps, w, 0] = jnp.where(head0_b, q, pen)
                qnext_ref[ps, w, 1] = jnp.where(head0_b, pen, q)

    @pl.when(u == 0)
    def _():
        for n in range(nblocks):
            rows = slice(n * blk, (n + 1) * blk)
            k = k_ref[0, rows, :]
            v = v_ref[0, rows, :]
            kaug_ref[0, rows, :] = jnp.where(head0_b, k, jnp.where(lane_b == HEAD_DIM + n, one, zero))
            kaug_ref[1, rows, :] = jnp.where(head0_b, jnp.where(lane_b == n, one, zero), k)
            vaug_ref[0, rows, :] = jnp.where(head0_b, v, one)
            vaug_ref[1, rows, :] = jnp.where(head0_b, one, v)
            kbar_ref[n:n + 1, :] = jnp.sum(k.astype(F32), axis=0, keepdims=True) / blk
        build_queries(qlo_ref, qhi_ref, 0)

    qaug_ref[...] = qnext_ref[...]
    build_queries(qlon_ref, qhin_ref, jnp.minimum(u + 1, pl.num_programs(2) - 1))

    r_iota = lax.broadcasted_iota(jnp.int32, (blk, blk), 0)
    c_iota = lax.broadcasted_iota(jnp.int32, (blk, blk), 1)
    causal = c_iota <= r_iota

    def unit(first, t):
        w = (t >= first).astype(jnp.int32)
        return w, pl.multiple_of((t - w * first) * blk, blk)

    def scores(ps, w, e, start):
        kblk = kaug_ref[e, pl.ds(start, blk), :]
        return LOG2_E * lax.dot_general(qaug_ref[ps, w, e], kblk, NT_DIMS,
                                        preferred_element_type=F32)

    def fold(s):
        return jnp.maximum(s[:, :LANES], s[:, LANES:])

    def own_start(first, w):
        return pl.multiple_of((first, npast - first)[w] * blk, blk)

    nunits = npast + 2

    def pass1_unit(ps, first, idx):
        if idx < 2:
            for e in range(2):
                s = jnp.where(causal, scores(ps, idx, e, own_start(first, idx)), NEG_BIG)
                s_ref[ps, e, npast + idx] = s
                mx_ref[ps, idx, e] = fold(s)
        else:
            w, start = unit(first, idx - 2)
            for e in range(2):
                s = scores(ps, w, e, start)
                s_ref[ps, e, idx - 2] = s
                mx_ref[ps, w, e] = jnp.maximum(mx_ref[ps, w, e], fold(s))

    def row_max(ps):
        for w in range(2):
            for e in range(2):
                m = jnp.max(mx_ref[ps, w, e], axis=1, keepdims=True)
                m_ref[ps, w, e] = jnp.broadcast_to(m, (blk, LANES))

    def pv(ps, w, e, t, start):
        m = m_ref[ps, w, e]
        prob = jnp.exp2(s_ref[ps, e, t] - jnp.concatenate([m, m], axis=1)).astype(BF16)
        return jnp.dot(prob, vaug_ref[e, pl.ds(start, blk), :], preferred_element_type=F32)

    def pass2_unit(ps, first, idx):
        if idx < 2:
            for e in range(2):
                acc_ref[ps, idx, e] = pv(ps, idx, e, npast + idx, own_start(first, idx))
        else:
            w, start = unit(first, idx - 2)
            for e in range(2):
                acc_ref[ps, w, e] += pv(ps, w, e, idx - 2, start)

    def finish(ps):
        targets = ((olo_ref, ps), (ohi_ref, MOBA_PAIRS - 1 - ps))
        for w, (o_ref, half) in enumerate(targets):
            a0 = acc_ref[ps, w, 0]
            a1 = acc_ref[ps, w, 1]
            out = jnp.where(head0, a0 / pltpu.roll(a0, HEAD_DIM, 1), a1 / pltpu.roll(a1, HEAD_DIM, 1))
            o_ref[0, half * blk:(half + 1) * blk, :] = out.astype(BF16)

    firsts = [MOBA_PAIRS * u + ps for ps in range(MOBA_PAIRS)]
    for ps in range(MOBA_PAIRS):
        for idx in range(nunits):
            pass1_unit(ps, firsts[ps], idx)
    for ps in range(MOBA_PAIRS):
        row_max(ps)
        for idx in range(nunits):
            pass2_unit(ps, firsts[ps], idx)
    for ps in range(MOBA_PAIRS):
        finish(ps)


def _moba(p):
    b, s, _ = p.shape
    nblocks = s // MOBA_BLOCK
    blk = MOBA_BLOCK
    rows = MOBA_PAIRS * blk
    steps = nblocks // 2 // MOBA_PAIRS
    assert nblocks <= HEAD_DIM and nblocks % (2 * MOBA_PAIRS) == 0
    assert nblocks % SUBLANES == 0
    nxt = lambda i: jnp.minimum(i + 1, steps - 1)
    qspec = lambda row_block: pl.BlockSpec(
        (1, rows, LANES), lambda bi, hp, i: (bi, row_block(i), hp))
    kspec = pl.BlockSpec((1, s, LANES), lambda bi, hp, i: (bi, 0, HEAD_PAIRS + hp))
    vspec = pl.BlockSpec((1, s, LANES), lambda bi, hp, i: (bi, 0, 2 * HEAD_PAIRS + hp))
    half = jax.ShapeDtypeStruct((b, s // 2, WIDTH), BF16)
    last = 2 * steps - 1
    return pl.pallas_call(
        functools.partial(_moba_kernel, nblocks=nblocks),
        grid=(b, HEAD_PAIRS, steps),
        in_specs=[
            qspec(lambda i: i), qspec(lambda i: last - i),
            qspec(nxt), qspec(lambda i: last - nxt(i)),
            kspec, vspec,
        ],
        out_specs=[
            pl.BlockSpec((1, rows, LANES), lambda bi, hp, i: (bi, i, hp)),
            pl.BlockSpec((1, rows, LANES), lambda bi, hp, i: (bi, steps - 1 - i, hp)),
        ],
        out_shape=[half, half],
        scratch_shapes=[
            pltpu.VMEM((2, s, LANES), BF16),
            pltpu.VMEM((2, s, LANES), BF16),
            pltpu.VMEM((nblocks, LANES), F32),
            pltpu.VMEM((MOBA_PAIRS, 2, 2, blk, LANES), BF16),
            pltpu.VMEM((MOBA_PAIRS, 2, 2, blk, LANES), BF16),
            pltpu.VMEM((MOBA_PAIRS, 2, nblocks + 1, blk, blk), F32),
            pltpu.VMEM((MOBA_PAIRS, 2, 2, blk, LANES), F32),
            pltpu.VMEM((MOBA_PAIRS, 2, 2, blk, LANES), F32),
            pltpu.VMEM((MOBA_PAIRS, 2, 2, blk, LANES), F32),
        ],
        compiler_params=pltpu.CompilerParams(
            dimension_semantics=("arbitrary", "arbitrary", "arbitrary"),
            vmem_limit_bytes=VMEM_LIMIT),
        name="moba",
    )(p, p, p, p, p, p)


def _retention_tables():
    c = RET_CHUNK
    log_gamma = jnp.log(1.0 - 2.0 ** (-5.0 - jnp.arange(RET_HEADS, dtype=F32)))
    pos = jnp.arange(c, dtype=F32)
    diff = pos[:, None] - pos[None, :]
    inner = jnp.where(diff[None] >= 0,
                      jnp.exp(jnp.maximum(diff, 0.0)[None] * log_gamma[:, None, None]), 0.0)
    cross = jnp.exp((pos + 1.0)[None, :] * log_gamma[:, None])
    sdec = jnp.exp((c - 1.0 - pos)[None, :] * log_gamma[:, None])
    chunk = jnp.exp(c * log_gamma)

    def lanes(t):
        t = jnp.repeat(t[:, :, None], HEAD_DIM, axis=2)
        t = t.reshape(HEAD_PAIRS, 2, c, HEAD_DIM).transpose(0, 2, 1, 3)
        return t.reshape(HEAD_PAIRS, c, LANES)

    same_head = (jnp.arange(LANES)[:, None] // HEAD_DIM) == (jnp.arange(LANES)[None, :] // HEAD_DIM)
    bd = same_head.astype(F32)
    kdec = jnp.repeat(chunk.reshape(HEAD_PAIRS, 2), HEAD_DIM, axis=1)[:, :, None] * bd[None]
    return inner, lanes(cross), lanes(sdec), kdec, bd


def _retention_kernel(q_ref, k_ref, v_ref, g_ref, idec_ref, cdec_ref, sdec_ref, kdec_ref, bd_ref,
                      gain_ref, o_ref, state_ref):
    lane = lax.broadcasted_iota(jnp.int32, (RET_CHUNK, LANES), 1)
    head0 = lane < HEAD_DIM
    head0_b = lane.astype(F32).astype(BF16) < HEAD_DIM
    zero = jnp.zeros((RET_CHUNK, LANES), BF16)

    @pl.when(pl.program_id(2) == 0)
    def _():
        state_ref[...] = jnp.zeros_like(state_ref)

    def head_mean(t):
        s0 = jnp.sum(jnp.where(head0, t, 0.0), axis=1, keepdims=True)
        s1 = jnp.sum(jnp.where(head0, 0.0, t), axis=1, keepdims=True)
        return jnp.where(head0, s0, s1) / HEAD_DIM

    state = state_ref[...]
    for j in range(RET_STEP_CHUNKS):
        rows = slice(j * RET_CHUNK, (j + 1) * RET_CHUNK)
        q = q_ref[0, rows, :]
        k = k_ref[0, rows, :]
        v = v_ref[0, rows, :]
        o_cross = jnp.dot(q, state.astype(BF16), preferred_element_type=F32) * cdec_ref[0]
        parts = []
        for e in range(2):
            qe = jnp.where(head0_b, q, zero) if e == 0 else jnp.where(head0_b, zero, q)
            s = lax.dot_general(qe, k, NT_DIMS, preferred_element_type=F32) * idec_ref[e]
            parts.append(jnp.dot(s.astype(BF16), v, preferred_element_type=F32))
        o = jnp.where(head0, parts[0], parts[1]) + o_cross

        kd = (k.astype(F32) * sdec_ref[0]).astype(BF16)
        upd = lax.dot_general(kd, v, TN_DIMS, preferred_element_type=F32)
        state = state * kdec_ref[0] + upd * bd_ref[...]

        d = o - head_mean(o)
        on = d * lax.rsqrt(head_mean(d * d) + GN_EPS)
        g = g_ref[0, rows, :].astype(F32)
        y = g * (1.0 / (1.0 + jnp.exp(-g))) * on * gain_ref[...]
        o_ref[0, rows, :] = y.astype(BF16)
    state_ref[...] = state


def _retention(p, gn_gain, tables):
    b, s, _ = p.shape
    c = RET_CHUNK
    inner, cross, sdec, kdec, bd = tables

    rows = RET_STEP_CHUNKS * c

    def col(group):
        return pl.BlockSpec((1, rows, LANES), lambda bi, hp, ci: (bi, ci, group * HEAD_PAIRS + hp))

    pair = lambda shape: pl.BlockSpec(shape, lambda bi, hp, ci: (hp, 0, 0))
    return pl.pallas_call(
        _retention_kernel,
        grid=(b, HEAD_PAIRS, s // rows),
        in_specs=[
            col(3), col(4), col(5), col(6),
            pl.BlockSpec((2, c, c), lambda bi, hp, ci: (hp, 0, 0)),
            pair((1, c, LANES)), pair((1, c, LANES)), pair((1, LANES, LANES)),
            _resident((LANES, LANES)),
            pl.BlockSpec((1, LANES), lambda bi, hp, ci: (0, hp)),
        ],
        out_specs=pl.BlockSpec((1, rows, LANES), lambda bi, hp, ci: (bi, ci, hp)),
        out_shape=jax.ShapeDtypeStruct((b, s, WIDTH), BF16),
        scratch_shapes=[pltpu.VMEM((LANES, LANES), F32)],
        compiler_params=pltpu.CompilerParams(
            dimension_semantics=("arbitrary", "arbitrary", "arbitrary"),
            vmem_limit_bytes=VMEM_LIMIT),
        name="retention",
    )(p, p, p, p, inner, cross, sdec, kdec, bd, gn_gain.reshape(1, WIDTH))


def _ffn_kernel(x_ref, ya_lo_ref, ya_hi_ref, yr_ref, woa_ref, wor_ref, ln2_ref, wg_ref, wu_ref,
                cw_ref, cb_ref, wd_ref, lnf_ref, o_ref, x1_ref, h2_ref, acc_ref, carry_ref, gu_ref,
                *, final_norm):
    tm = x_ref.shape[1]

    @pl.when(pl.program_id(1) == 0)
    def _():
        carry_ref[...] = jnp.zeros_like(carry_ref)

    first_half = (pl.program_id(1) < pl.num_programs(1) // 2).astype(F32)
    pick_lo = jnp.full((tm, WIDTH), first_half, F32).astype(BF16)
    ya = ya_lo_ref[0] * pick_lo + ya_hi_ref[0] * (1 - pick_lo)
    x1 = (x_ref[0]
          + jnp.dot(ya, woa_ref[...], preferred_element_type=F32)
          + jnp.dot(yr_ref[0], wor_ref[...], preferred_element_type=F32))
    x1_ref[...] = x1
    h2_ref[...] = _rmsnorm(x1, ln2_ref[...]).astype(BF16)
    rows = lax.broadcasted_iota(jnp.int32, (tm, FFN_CHUNK), 0)
    row0 = rows == 0
    row1 = rows == 1

    def up_proj(c):
        h2 = h2_ref[...]
        cols = slice(c * FFN_CHUNK, (c + 1) * FFN_CHUNK)
        gu_ref[c % FFN_SLOTS, 0] = jnp.dot(h2, wg_ref[:, cols], preferred_element_type=F32)
        gu_ref[c % FFN_SLOTS, 1] = jnp.dot(h2, wu_ref[:, cols], preferred_element_type=F32)

    def activation(c):
        g = gu_ref[c % FFN_SLOTS, 0]
        u = gu_ref[c % FFN_SLOTS, 1]
        prev = carry_ref[c]
        carry_ref[c] = g[tm - SUBLANES:, :]
        p1 = prev[SUBLANES - 1:SUBLANES, :]
        p2 = prev[SUBLANES - 2:SUBLANES - 1, :]
        g1 = jnp.where(row0, p1, pltpu.roll(g, 1, 0))
        g2 = jnp.where(row0, p2, jnp.where(row1, p1, pltpu.roll(g, 2, 0)))
        cols = slice(c * FFN_CHUNK, (c + 1) * FFN_CHUNK)
        cw = cw_ref[:, cols]
        gc = cw[0:1, :] * g2 + cw[1:2, :] * g1 + cw[2:3, :] * g + cb_ref[:, cols]
        return (gc * (1.0 / (1.0 + jnp.exp(-gc))) * u).astype(BF16)

    for c in range(min(FFN_SLOTS, FFN_NCHUNK)):
        up_proj(c)
    for c0 in range(0, FFN_NCHUNK, 2):
        part = None
        for c in range(c0, min(c0 + 2, FFN_NCHUNK)):
            act = activation(c)
            if c + FFN_SLOTS < FFN_NCHUNK:
                up_proj(c + FFN_SLOTS)
            down = jnp.dot(act, wd_ref[c * FFN_CHUNK:(c + 1) * FFN_CHUNK, :],
                           preferred_element_type=F32)
            part = down if part is None else part + down
        acc_ref[...] = part if c0 == 0 else acc_ref[...] + part
    x2 = x1_ref[...] + acc_ref[...]
    if final_norm:
        x2 = _rmsnorm(x2, lnf_ref[...])
    o_ref[0] = x2


def _ffn(x, ya_lo, ya_hi, yr, woa, wor, ln2, wg, wu, cw, cb, wd, ln_f, final_norm):
    b, s, _ = x.shape
    tm = FFN_TILE
    nhalf = s // tm // 2
    tok = lambda width: pl.BlockSpec((1, tm, width), lambda bi, ti: (bi, ti, 0))
    lo = pl.BlockSpec((1, tm, WIDTH), lambda bi, ti: (bi, jnp.minimum(ti, nhalf - 1), 0))
    hi = pl.BlockSpec((1, tm, WIDTH), lambda bi, ti: (bi, jnp.maximum(ti - nhalf, 0), 0))
    return pl.pallas_call(
        functools.partial(_ffn_kernel, final_norm=final_norm),
        grid=(b, s // tm),
        in_specs=[
            tok(D_MODEL), lo, hi, tok(WIDTH),
            _resident((WIDTH, D_MODEL)), _resident((WIDTH, D_MODEL)),
            _resident((1, D_MODEL)),
            _resident((D_MODEL, D_FF)), _resident((D_MODEL, D_FF)),
            _resident((3, D_FF)), _resident((1, D_FF)),
            _resident((D_FF, D_MODEL)),
            _resident((1, D_MODEL)),
        ],
        out_specs=tok(D_MODEL),
        out_shape=jax.ShapeDtypeStruct((b, s, D_MODEL), F32),
        scratch_shapes=[
            pltpu.VMEM((tm, D_MODEL), F32),
            pltpu.VMEM((tm, D_MODEL), BF16),
            pltpu.VMEM((tm, D_MODEL), F32),
            pltpu.VMEM((FFN_NCHUNK, SUBLANES, FFN_CHUNK), F32),
            pltpu.VMEM((FFN_SLOTS, 2, tm, FFN_CHUNK), F32),
        ],
        compiler_params=pltpu.CompilerParams(
            dimension_semantics=("arbitrary", "arbitrary"), vmem_limit_bytes=VMEM_LIMIT),
        name="ffn",
    )(x, ya_lo, ya_hi, yr, woa, wor, ln2.reshape(1, D_MODEL), wg, wu, cw, cb.reshape(1, D_FF),
      wd,
      ln_f.reshape(1, D_MODEL))


def kernel(x, ln1, w_in, gn_gain, w_out, ln2, w_up, conv_w, conv_b, w_down, ln_f):
    b, s, d = x.shape
    depth = w_in.shape[0]
    assert d == D_MODEL
    assert s % max(IN_TILE, 2 * FFN_TILE, MOBA_BLOCK, RET_STEP_CHUNKS * RET_CHUNK) == 0

    inv_a = ROPE_THETA ** (-jnp.arange(ROPE_DIM // 2, dtype=F32) / (ROPE_DIM // 2))
    inv_r = 1.0 / (RET_ROPE_THETA ** jnp.linspace(0.0, 1.0, HEAD_DIM // 2, dtype=F32))
    tabs_a = _rope_tables(s, inv_a, ROPE_DIM // 2)
    tabs_r = _rope_tables(s, inv_r, HEAD_DIM // 2)
    ret_tables = _retention_tables()
    scale = HEAD_DIM ** -0.5
    col_scale = jnp.ones((IN_COLS,), F32).at[0:WIDTH].set(scale).at[4 * WIDTH:5 * WIDTH].set(scale)

    for l in range(depth):
        w_in_l = (w_in[l] * col_scale[None, :]).astype(BF16)
        p = _inproj(x, ln1[l], w_in_l, tabs_a, tabs_r)
        ya_lo, ya_hi = _moba(p)
        yr = _retention(p, gn_gain[l], ret_tables)
        wo = w_out[l].astype(BF16)
        x = _ffn(x, ya_lo, ya_hi, yr, wo[:WIDTH], wo[WIDTH:], ln2[l],
                 w_up[l][:, :D_FF].astype(BF16), w_up[l][:, D_FF:].astype(BF16),
                 conv_w[l], conv_b[l], w_down[l].astype(BF16),
                 ln_f, final_norm=(l == depth - 1))
    return x
```

```python
import functools

import jax
import jax.numpy as jnp
from jax import lax
from jax.experimental import pallas as pl
from jax.experimental.pallas import tpu as pltpu

D_MODEL = 1024
HEAD_DIM = 64
MOBA_HEADS = 8
RET_HEADS = 8
WIDTH = 512
IN_COLS = 7 * WIDTH
MOBA_BLOCK = 256
MOBA_TOPK = 3
ROPE_THETA = 500000.0
ROPE_DIM = HEAD_DIM // 4
RET_ROPE_THETA = 10000.0
RET_CHUNK = 256
D_FF = 2816
NORM_EPS = 1e-6
GN_EPS = 1e-5
NEG_BIG = -1e9

LANES = 128
HEAD_PAIRS = WIDTH // LANES
SUBLANES = 8
VMEM_LIMIT = 56 * 1024 * 1024

IN_TILE = 1024
FFN_TILE = 512
FFN_CHUNK = 256
FFN_NCHUNK = D_FF // FFN_CHUNK
FFN_SLOTS = 9
RET_STEP_CHUNKS = 16
MOBA_PAIRS = 4
LOG2_E = 1.4426950408889634

F32 = jnp.float32
BF16 = jnp.bfloat16
NT_DIMS = (((1,), (1,)), ((), ()))
TN_DIMS = (((0,), (0,)), ((), ()))


def _rmsnorm(x, g):
    return x * lax.rsqrt(jnp.mean(x * x, axis=-1, keepdims=True) + NORM_EPS) * g


def _resident(shape):
    zeros = (0,) * len(shape)
    return pl.BlockSpec(shape, lambda *_: zeros, pipeline_mode=pl.Buffered(1))


def _rope_tables(seq, inv_freq, half):
    ang = jnp.arange(seq, dtype=F32)[:, None] * inv_freq[None, :]
    cos, sin = jnp.cos(ang), jnp.sin(ang)
    pad = HEAD_DIM - 2 * half
    ones = jnp.ones((seq, pad), F32)
    zeros = jnp.zeros((seq, pad), F32)
    zh = jnp.zeros((seq, half), F32)
    c = jnp.concatenate([cos, cos, ones], axis=1)
    su = jnp.concatenate([-sin, zh, zeros], axis=1)
    sd = jnp.concatenate([zh, sin, zeros], axis=1)
    two = lambda t: jnp.concatenate([t, t], axis=1)
    return two(c), two(su), two(sd)


def _inproj_kernel(x_ref, ln_ref, w_ref, ca_ref, sau_ref, sad_ref, cr_ref, sru_ref, srd_ref,
                   p_ref):
    h = _rmsnorm(x_ref[0], ln_ref[...]).astype(BF16)

    def rope(y, c, su, sd, half):
        outs = []
        for g in range(WIDTH // LANES):
            yg = y[:, g * LANES:(g + 1) * LANES]
            up = pltpu.roll(yg, LANES - half, 1)
            dn = pltpu.roll(yg, half, 1)
            outs.append(yg * c + up * su + dn * sd)
        return jnp.concatenate(outs, axis=1)

    for slab in range(IN_COLS // WIDTH):
        cols = slice(slab * WIDTH, (slab + 1) * WIDTH)
        y = jnp.dot(h, w_ref[:, cols], preferred_element_type=F32)
        if slab in (0, 1):
            y = rope(y, ca_ref[...], sau_ref[...], sad_ref[...], ROPE_DIM // 2)
        elif slab in (3, 4):
            y = rope(y, cr_ref[...], sru_ref[...], srd_ref[...], HEAD_DIM // 2)
        p_ref[0, :, cols] = y.astype(BF16)


def _inproj(x, ln, w_bf16, tabs_a, tabs_r):
    b, s, _ = x.shape
    tm = IN_TILE
    tab = pl.BlockSpec((tm, LANES), lambda si, bi: (si, 0))
    return pl.pallas_call(
        _inproj_kernel,
        grid=(s // tm, b),
        in_specs=[
            pl.BlockSpec((1, tm, D_MODEL), lambda si, bi: (bi, si, 0)),
            _resident((1, D_MODEL)),
            _resident((D_MODEL, IN_COLS)),
            tab, tab, tab, tab, tab, tab,
        ],
        out_specs=pl.BlockSpec((1, tm, IN_COLS), lambda si, bi: (bi, si, 0)),
        out_shape=jax.ShapeDtypeStruct((b, s, IN_COLS), BF16),
        compiler_params=pltpu.CompilerParams(
            dimension_semantics=("arbitrary", "arbitrary"), vmem_limit_bytes=VMEM_LIMIT),
        name="inproj",
    )(x, ln.reshape(1, D_MODEL), w_bf16, *tabs_a, *tabs_r)


def _moba_kernel(qlo_ref, qhi_ref, qlon_ref, qhin_ref, k_ref, v_ref, olo_ref, ohi_ref,
                 kaug_ref, vaug_ref, kbar_ref, qaug_ref, qnext_ref, s_ref, mx_ref, m_ref, acc_ref,
                 *, nblocks):
    u = pl.program_id(2)
    blk = MOBA_BLOCK
    npast = nblocks - 1
    lane = lax.broadcasted_iota(jnp.int32, (blk, LANES), 1)
    head0 = lane < HEAD_DIM
    lane_b = lane.astype(F32).astype(BF16)
    head0_b = lane_b < HEAD_DIM
    one = jnp.ones((blk, LANES), BF16)
    zero = jnp.zeros((blk, LANES), BF16)

    def build_queries(lo_ref, hi_ref, step):
        kbar = kbar_ref[...]
        kb_hi = kbar.astype(BF16)
        kb_lo = (kbar - kb_hi.astype(F32)).astype(BF16)
        kb = jnp.concatenate([kb_hi, kb_lo], axis=0)
        n_iota = lax.broadcasted_iota(jnp.int32, (nblocks, blk), 0)
        fill = jnp.zeros((HEAD_DIM - nblocks, blk), F32)
        for ps in range(MOBA_PAIRS):
            first = MOBA_PAIRS * step + ps
            sources = ((lo_ref, ps, first), (hi_ref, MOBA_PAIRS - 1 - ps, npast - first))
            for w, (q_ref, half, i) in enumerate(sources):
                q = q_ref[0, half * blk:(half + 1) * blk, :]
                past = n_iota < i
                pens = []
                for e in range(2):
                    qe = jnp.where(head0_b, q, zero) if e == 0 else jnp.where(head0_b, zero, q)
                    g2 = lax.dot_general(kb, qe, NT_DIMS, preferred_element_type=F32)
                    g = jnp.where(past, g2[:nblocks] + g2[nblocks:], NEG_BIG)
                    cnt = jnp.zeros((nblocks, blk), F32)
                    for m in range(nblocks):
                        row = g[m:m + 1, :]
                        ahead = (row > g) | ((row == g) & (n_iota > m))
                        cnt = cnt + jnp.where(ahead, 1.0, 0.0)
                    keep = ((cnt < MOBA_TOPK) & past) | (n_iota == i)
                    pens.append(jnp.where(keep, 0.0, NEG_BIG))
                pen_t = jnp.concatenate([pens[1], fill, pens[0], fill], axis=0)
                pen = pen_t.T.astype(BF16)
                qnext_ref[ps, w, 0] = jnp.where(head0_b, q, pen)
                qnext_ref[ps, w, 1] = jnp.where(head0_b, pen, q)

    @pl.when(u == 0)
    def _():
        for n in range(nblocks):
            rows = slice(n * blk, (n + 1) * blk)
            k = k_ref[0, rows, :]
            v = v_ref[0, rows, :]
            kaug_ref[0, rows, :] = jnp.where(head0_b, k, jnp.where(lane_b == HEAD_DIM + n, one, zero))
            kaug_ref[1, rows, :] = jnp.where(head0_b, jnp.where(lane_b == n, one, zero), k)
            vaug_ref[0, rows, :] = jnp.where(head0_b, v, one)
            vaug_ref[1, rows, :] = jnp.where(head0_b, one, v)
            kbar_ref[n:n + 1, :] = jnp.sum(k.astype(F32), axis=0, keepdims=True) / blk
        build_queries(qlo_ref, qhi_ref, 0)

    qaug_ref[...] = qnext_ref[...]
    build_queries(qlon_ref, qhin_ref, jnp.minimum(u + 1, pl.num_programs(2) - 1))

    r_iota = lax.broadcasted_iota(jnp.int32, (blk, blk), 0)
    c_iota = lax.broadcasted_iota(jnp.int32, (blk, blk), 1)
    causal = c_iota <= r_iota

    def unit(first, t):
        w = (t >= first).astype(jnp.int32)
        return w, pl.multiple_of((t - w * first) * blk, blk)

    def scores(ps, w, e, start):
        kblk = kaug_ref[e, pl.ds(start, blk), :]
        return LOG2_E * lax.dot_general(qaug_ref[ps, w, e], kblk, NT_DIMS,
                                        preferred_element_type=F32)

    def fold(s):
        return jnp.maximum(s[:, :LANES], s[:, LANES:])

    def own_start(first, w):
        return pl.multiple_of((first, npast - first)[w] * blk, blk)

    nunits = npast + 2

    def pass1_unit(ps, first, idx):
        if idx < 2:
            for e in range(2):
                s = jnp.where(causal, scores(ps, idx, e, own_start(first, idx)), NEG_BIG)
                s_ref[ps, e, npast + idx] = s
                mx_ref[ps, idx, e] = fold(s)
        else:
            w, start = unit(first, idx - 2)
            for e in range(2):
                s = scores(ps, w, e, start)
                s_ref[ps, e, idx - 2] = s
                mx_ref[ps, w, e] = jnp.maximum(mx_ref[ps, w, e], fold(s))

    def row_max(ps):
        for w in range(2):
            for e in range(2):
                m = jnp.max(mx_ref[ps, w, e], axis=1, keepdims=True)
                m_ref[ps, w, e] = jnp.broadcast_to(m, (blk, LANES))

    def pv(ps, w, e, t, start):
        m = m_ref[ps, w, e]
        prob = jnp.exp2(s_ref[ps, e, t] - jnp.concatenate([m, m], axis=1)).astype(BF16)
        return jnp.dot(prob, vaug_ref[e, pl.ds(start, blk), :], preferred_element_type=F32)

    def pass2_unit(ps, first, idx):
        if idx < 2:
            for e in range(2):
                acc_ref[ps, idx, e] = pv(ps, idx, e, npast + idx, own_start(first, idx))
        else:
            w, start = unit(first, idx - 2)
            for e in range(2):
                acc_ref[ps, w, e] += pv(ps, w, e, idx - 2, start)

    def finish(ps):
        targets = ((olo_ref, ps), (ohi_ref, MOBA_PAIRS - 1 - ps))
        for w, (o_ref, half) in enumerate(targets):
            a0 = acc_ref[ps, w, 0]
            a1 = acc_ref[ps, w, 1]
            out = jnp.where(head0, a0 / pltpu.roll(a0, HEAD_DIM, 1), a1 / pltpu.roll(a1, HEAD_DIM, 1))
            o_ref[0, half * blk:(half + 1) * blk, :] = out.astype(BF16)

    firsts = [MOBA_PAIRS * u + ps for ps in range(MOBA_PAIRS)]
    for ps in range(MOBA_PAIRS):
        for idx in range(nunits):
            pass1_unit(ps, firsts[ps], idx)
    for ps in range(MOBA_PAIRS):
        row_max(ps)
        for idx in range(nunits):
            pass2_unit(ps, firsts[ps], idx)
    for ps in range(MOBA_PAIRS):
        finish(ps)


def _moba(p):
    b, s, _ = p.shape
    nblocks = s // MOBA_BLOCK
    blk = MOBA_BLOCK
    rows = MOBA_PAIRS * blk
    steps = nblocks // 2 // MOBA_PAIRS
    assert nblocks <= HEAD_DIM and nblocks % (2 * MOBA_PAIRS) == 0
    assert nblocks % SUBLANES == 0
    nxt = lambda i: jnp.minimum(i + 1, steps - 1)
    qspec = lambda row_block: pl.BlockSpec(
        (1, rows, LANES), lambda bi, hp, i: (bi, row_block(i), hp))
    kspec = pl.BlockSpec((1, s, LANES), lambda bi, hp, i: (bi, 0, HEAD_PAIRS + hp))
    vspec = pl.BlockSpec((1, s, LANES), lambda bi, hp, i: (bi, 0, 2 * HEAD_PAIRS + hp))
    half = jax.ShapeDtypeStruct((b, s // 2, WIDTH), BF16)
    last = 2 * steps - 1
    return pl.pallas_call(
        functools.partial(_moba_kernel, nblocks=nblocks),
        grid=(b, HEAD_PAIRS, steps),
        in_specs=[
            qspec(lambda i: i), qspec(lambda i: last - i),
            qspec(nxt), qspec(lambda i: last - nxt(i)),
            kspec, vspec,
        ],
        out_specs=[
            pl.BlockSpec((1, rows, LANES), lambda bi, hp, i: (bi, i, hp)),
            pl.BlockSpec((1, rows, LANES), lambda bi, hp, i: (bi, steps - 1 - i, hp)),
        ],
        out_shape=[half, half],
        scratch_shapes=[
            pltpu.VMEM((2, s, LANES), BF16),
            pltpu.VMEM((2, s, LANES), BF16),
            pltpu.VMEM((nblocks, LANES), F32),
            pltpu.VMEM((MOBA_PAIRS, 2, 2, blk, LANES), BF16),
            pltpu.VMEM((MOBA_PAIRS, 2, 2, blk, LANES), BF16),
            pltpu.VMEM((MOBA_PAIRS, 2, nblocks + 1, blk, blk), F32),
            pltpu.VMEM((MOBA_PAIRS, 2, 2, blk, LANES), F32),
            pltpu.VMEM((MOBA_PAIRS, 2, 2, blk, LANES), F32),
            pltpu.VMEM((MOBA_PAIRS, 2, 2, blk, LANES), F32),
        ],
        compiler_params=pltpu.CompilerParams(
            dimension_semantics=("arbitrary", "arbitrary", "arbitrary"),
            vmem_limit_bytes=VMEM_LIMIT),
        name="moba",
    )(p, p, p, p, p, p)


def _retention_tables():
    c = RET_CHUNK
    log_gamma = jnp.log(1.0 - 2.0 ** (-5.0 - jnp.arange(RET_HEADS, dtype=F32)))
    pos = jnp.arange(c, dtype=F32)
    diff = pos[:, None] - pos[None, :]
    inner = jnp.where(diff[None] >= 0,
                      jnp.exp(jnp.maximum(diff, 0.0)[None] * log_gamma[:, None, None]), 0.0)
    cross = jnp.exp((pos + 1.0)[None, :] * log_gamma[:, None])
    sdec = jnp.exp((c - 1.0 - pos)[None, :] * log_gamma[:, None])
    chunk = jnp.exp(c * log_gamma)

    def lanes(t):
        t = jnp.repeat(t[:, :, None], HEAD_DIM, axis=2)
        t = t.reshape(HEAD_PAIRS, 2, c, HEAD_DIM).transpose(0, 2, 1, 3)
        return t.reshape(HEAD_PAIRS, c, LANES)

    same_head = (jnp.arange(LANES)[:, None] // HEAD_DIM) == (jnp.arange(LANES)[None, :] // HEAD_DIM)
    bd = same_head.astype(F32)
    kdec = jnp.repeat(chunk.reshape(HEAD_PAIRS, 2), HEAD_DIM, axis=1)[:, :, None] * bd[None]
    return inner, lanes(cross), lanes(sdec), kdec, bd


def _retention_kernel(q_ref, k_ref, v_ref, g_ref, idec_ref, cdec_ref, sdec_ref, kdec_ref, bd_ref,
                      gain_ref, o_ref, state_ref):
    lane = lax.broadcasted_iota(jnp.int32, (RET_CHUNK, LANES), 1)
    head0 = lane < HEAD_DIM
    head0_b = lane.astype(F32).astype(BF16) < HEAD_DIM
    zero = jnp.zeros((RET_CHUNK, LANES), BF16)

    @pl.when(pl.program_id(2) == 0)
    def _():
        state_ref[...] = jnp.zeros_like(state_ref)

    def head_mean(t):
        s0 = jnp.sum(jnp.where(head0, t, 0.0), axis=1, keepdims=True)
        s1 = jnp.sum(jnp.where(head0, 0.0, t), axis=1, keepdims=True)
        return jnp.where(head0, s0, s1) / HEAD_DIM

    state = state_ref[...]
    for j in range(RET_STEP_CHUNKS):
        rows = slice(j * RET_CHUNK, (j + 1) * RET_CHUNK)
        q = q_ref[0, rows, :]
        k = k_ref[0, rows, :]
        v = v_ref[0, rows, :]
        o_cross = jnp.dot(q, state.astype(BF16), preferred_element_type=F32) * cdec_ref[0]
        parts = []
        for e in range(2):
            qe = jnp.where(head0_b, q, zero) if e == 0 else jnp.where(head0_b, zero, q)
            s = lax.dot_general(qe, k, NT_DIMS, preferred_element_type=F32) * idec_ref[e]
            parts.append(jnp.dot(s.astype(BF16), v, preferred_element_type=F32))
        o = jnp.where(head0, parts[0], parts[1]) + o_cross

        kd = (k.astype(F32) * sdec_ref[0]).astype(BF16)
        upd = lax.dot_general(kd, v, TN_DIMS, preferred_element_type=F32)
        state = state * kdec_ref[0] + upd * bd_ref[...]

        d = o - head_mean(o)
        on = d * lax.rsqrt(head_mean(d * d) + GN_EPS)
        g = g_ref[0, rows, :].astype(F32)
        y = g * (1.0 / (1.0 + jnp.exp(-g))) * on * gain_ref[...]
        o_ref[0, rows, :] = y.astype(BF16)
    state_ref[...] = state


def _retention(p, gn_gain, tables):
    b, s, _ = p.shape
    c = RET_CHUNK
    inner, cross, sdec, kdec, bd = tables

    rows = RET_STEP_CHUNKS * c

    def col(group):
        return pl.BlockSpec((1, rows, LANES), lambda bi, hp, ci: (bi, ci, group * HEAD_PAIRS + hp))

    pair = lambda shape: pl.BlockSpec(shape, lambda bi, hp, ci: (hp, 0, 0))
    return pl.pallas_call(
        _retention_kernel,
        grid=(b, HEAD_PAIRS, s // rows),
        in_specs=[
            col(3), col(4), col(5), col(6),
            pl.BlockSpec((2, c, c), lambda bi, hp, ci: (hp, 0, 0)),
            pair((1, c, LANES)), pair((1, c, LANES)), pair((1, LANES, LANES)),
            _resident((LANES, LANES)),
            pl.BlockSpec((1, LANES), lambda bi, hp, ci: (0, hp)),
        ],
        out_specs=pl.BlockSpec((1, rows, LANES), lambda bi, hp, ci: (bi, ci, hp)),
        out_shape=jax.ShapeDtypeStruct((b, s, WIDTH), BF16),
        scratch_shapes=[pltpu.VMEM((LANES, LANES), F32)],
        compiler_params=pltpu.CompilerParams(
            dimension_semantics=("arbitrary", "arbitrary", "arbitrary"),
            vmem_limit_bytes=VMEM_LIMIT),
        name="retention",
    )(p, p, p, p, inner, cross, sdec, kdec, bd, gn_gain.reshape(1, WIDTH))


def _ffn_kernel(x_ref, ya_lo_ref, ya_hi_ref, yr_ref, woa_ref, wor_ref, ln2_ref, wg_ref, wu_ref,
                cw_ref, cb_ref, wd_ref, lnf_ref, o_ref, x1_ref, h2_ref, acc_ref, carry_ref, gu_ref,
                *, final_norm):
    tm = x_ref.shape[1]

    @pl.when(pl.program_id(1) == 0)
    def _():
        carry_ref[...] = jnp.zeros_like(carry_ref)

    first_half = (pl.program_id(1) < pl.num_programs(1) // 2).astype(F32)
    pick_lo = jnp.full((tm, WIDTH), first_half, F32).astype(BF16)
    ya = ya_lo_ref[0] * pick_lo + ya_hi_ref[0] * (1 - pick_lo)
    x1 = (x_ref[0]
          + jnp.dot(ya, woa_ref[...], preferred_element_type=F32)
          + jnp.dot(yr_ref[0], wor_ref[...], preferred_element_type=F32))
    x1_ref[...] = x1
    h2_ref[...] = _rmsnorm(x1, ln2_ref[...]).astype(BF16)
    rows = lax.broadcasted_iota(jnp.int32, (tm, FFN_CHUNK), 0)
    row0 = rows == 0
    row1 = rows == 1

    def up_proj(c):
        h2 = h2_ref[...]
        cols = slice(c * FFN_CHUNK, (c + 1) * FFN_CHUNK)
        gu_ref[c % FFN_SLOTS, 0] = jnp.dot(h2, wg_ref[:, cols], preferred_element_type=F32)
        gu_ref[c % FFN_SLOTS, 1] = jnp.dot(h2, wu_ref[:, cols], preferred_element_type=F32)

    def activation(c):
        g = gu_ref[c % FFN_SLOTS, 0]
        u = gu_ref[c % FFN_SLOTS, 1]
        prev = carry_ref[c]
        carry_ref[c] = g[tm - SUBLANES:, :]
        p1 = prev[SUBLANES - 1:SUBLANES, :]
        p2 = prev[SUBLANES - 2:SUBLANES - 1, :]
        g1 = jnp.where(row0, p1, pltpu.roll(g, 1, 0))
        g2 = jnp.where(row0, p2, jnp.where(row1, p1, pltpu.roll(g, 2, 0)))
        cols = slice(c * FFN_CHUNK, (c + 1) * FFN_CHUNK)
        cw = cw_ref[:, cols]
        gc = cw[0:1, :] * g2 + cw[1:2, :] * g1 + cw[2:3, :] * g + cb_ref[:, cols]
        return (gc * (1.0 / (1.0 + jnp.exp(-gc))) * u).astype(BF16)

    for c in range(min(FFN_SLOTS, FFN_NCHUNK)):
        up_proj(c)
    for c0 in range(0, FFN_NCHUNK, 2):
        part = None
        for c in range(c0, min(c0 + 2, FFN_NCHUNK)):
            act = activation(c)
            if c + FFN_SLOTS < FFN_NCHUNK:
                up_proj(c + FFN_SLOTS)
            down = jnp.dot(act, wd_ref[c * FFN_CHUNK:(c + 1) * FFN_CHUNK, :],
                           preferred_element_type=F32)
            part = down if part is None else part + down
        acc_ref[...] = part if c0 == 0 else acc_ref[...] + part
    x2 = x1_ref[...] + acc_ref[...]
    if final_norm:
        x2 = _rmsnorm(x2, lnf_ref[...])
    o_ref[0] = x2


def _ffn(x, ya_lo, ya_hi, yr, woa, wor, ln2, wg, wu, cw, cb, wd, ln_f, final_norm):
    b, s, _ = x.shape
    tm = FFN_TILE
    nhalf = s // tm // 2
    tok = lambda width: pl.BlockSpec((1, tm, width), lambda bi, ti: (bi, ti, 0))
    lo = pl.BlockSpec((1, tm, WIDTH), lambda bi, ti: (bi, jnp.minimum(ti, nhalf - 1), 0))
    hi = pl.BlockSpec((1, tm, WIDTH), lambda bi, ti: (bi, jnp.maximum(ti - nhalf, 0), 0))
    return pl.pallas_call(
        functools.partial(_ffn_kernel, final_norm=final_norm),
        grid=(b, s // tm),
        in_specs=[
            tok(D_MODEL), lo, hi, tok(WIDTH),
            _resident((WIDTH, D_MODEL)), _resident((WIDTH, D_MODEL)),
            _resident((1, D_MODEL)),
            _resident((D_MODEL, D_FF)), _resident((D_MODEL, D_FF)),
            _resident((3, D_FF)), _resident((1, D_FF)),
            _resident((D_FF, D_MODEL)),
            _resident((1, D_MODEL)),
        ],
        out_specs=tok(D_MODEL),
        out_shape=jax.ShapeDtypeStruct((b, s, D_MODEL), F32),
        scratch_shapes=[
            pltpu.VMEM((tm, D_MODEL), F32),
            pltpu.VMEM((tm, D_MODEL), BF16),
            pltpu.VMEM((tm, D_MODEL), F32),
            pltpu.VMEM((FFN_NCHUNK, SUBLANES, FFN_CHUNK), F32),
            pltpu.VMEM((FFN_SLOTS, 2, tm, FFN_CHUNK), F32),
        ],
        compiler_params=pltpu.CompilerParams(
            dimension_semantics=("arbitrary", "arbitrary"), vmem_limit_bytes=VMEM_LIMIT),
        name="ffn",
    )(x, ya_lo, ya_hi, yr, woa, wor, ln2.reshape(1, D_MODEL), wg, wu, cw, cb.reshape(1, D_FF),
      wd,
      ln_f.reshape(1, D_MODEL))


def kernel(x, ln1, w_in, gn_gain, w_out, ln2, w_up, conv_w, conv_b, w_down, ln_f):
    b, s, d = x.shape
    depth = w_in.shape[0]
    assert d == D_MODEL
    assert s % max(IN_TILE, 2 * FFN_TILE, MOBA_BLOCK, RET_STEP_CHUNKS * RET_CHUNK) == 0

    inv_a = ROPE_THETA ** (-jnp.arange(ROPE_DIM // 2, dtype=F32) / (ROPE_DIM // 2))
    inv_r = 1.0 / (RET_ROPE_THETA ** jnp.linspace(0.0, 1.0, HEAD_DIM // 2, dtype=F32))
    tabs_a = _rope_tables(s, inv_a, ROPE_DIM // 2)
    tabs_r = _rope_tables(s, inv_r, HEAD_DIM // 2)
    ret_tables = _retention_tables()
    scale = HEAD_DIM ** -0.5
    col_scale = jnp.ones((IN_COLS,), F32).at[0:WIDTH].set(scale).at[4 * WIDTH:5 * WIDTH].set(scale)

    for l in range(depth):
        w_in_l = (w_in[l] * col_scale[None, :]).astype(BF16)
        p = _inproj(x, ln1[l], w_in_l, tabs_a, tabs_r)
        ya_lo, ya_hi = _moba(p)
        yr = _retention(p, gn_gain[l], ret_tables)
        wo = w_out[l].astype(BF16)
        x = _ffn(x, ya_lo, ya_hi, yr, wo[:WIDTH], wo[WIDTH:], ln2[l],
                 w_up[l][:, :D_FF].astype(BF16), w_up[l][:, D_FF:].astype(BF16),
                 conv_w[l], conv_b[l], w_down[l].astype(BF16),
                 ln_f, final_norm=(l == depth - 1))
    return x
```
